```python
import jax, jax.numpy as jnp
from jax import lax
import numpy as np

D_MODEL = 1024
BATCH = 16
SEQ = 4096
DEPTH = 1
DEC_BATCH = 128
DEC_SEQ = 4
PAST_LEN = 8192
PAGE_SIZE = 128

MIX_WIDTH = D_MODEL
N_HEADS = 8
HEAD_DIM = 64
N_KV_HEADS = 4
KV_GROUP = N_HEADS // N_KV_HEADS
ATT_WIDTH = N_HEADS * HEAD_DIM
KV_WIDTH = N_KV_HEADS * HEAD_DIM
MOBA_BLOCK = 256
MOBA_TOP_K = 3
Q_BLOCK = 32
GM_WIDTH = MIX_WIDTH - ATT_WIDTH
GM_GROUPS = 8
GM_GROUP_DIM = GM_WIDTH // GM_GROUPS
GM_CHUNK = 128
IN_COLS = ATT_WIDTH + 2 * KV_WIDTH + ATT_WIDTH + 3 * GM_WIDTH
NORM_EPS = 1e-6
LN_EPS = 1e-5

kernel_name = "hymba_moba_gmlp_decoder_step"


def rms_norm(x, g):
    xf = x.astype(jnp.float32)
    y = xf * lax.rsqrt(jnp.mean(xf * xf, axis=-1, keepdims=True) + NORM_EPS)
    return (y * g.astype(jnp.float32)).astype(x.dtype)


def group_layer_norm(v, g):
    shp = v.shape
    vf = v.astype(jnp.float32).reshape(shp[:-1] + (GM_GROUPS, GM_GROUP_DIM))
    mu = jnp.mean(vf, axis=-1, keepdims=True)
    var = jnp.mean(jnp.square(vf - mu), axis=-1, keepdims=True)
    y = ((vf - mu) * lax.rsqrt(var + LN_EPS)).reshape(shp)
    return (y * g.astype(jnp.float32)).astype(v.dtype)


def to_blocks(k):
    B, L = k.shape[:2]
    lp = -(-L // MOBA_BLOCK) * MOBA_BLOCK
    k = jnp.pad(k, ((0, 0), (0, lp - L), (0, 0), (0, 0)))
    return k.reshape(B, lp // MOBA_BLOCK, MOBA_BLOCK, N_KV_HEADS, HEAD_DIM).transpose(0, 3, 1, 2, 4)


def moba_queries(q, qpos, kb, vb, kmean, k_own, v_own, own_blk):
    B, KVH, G, Q, D = q.shape
    nb = kb.shape[2]
    n_sel = min(MOBA_TOP_K, nb)
    scale = D ** -0.5
    gate = jnp.einsum('bkgqd,bknd->bkgqn', q, kmean).astype(jnp.float32)
    gate = jnp.where(jnp.arange(nb) < own_blk, gate, -jnp.inf)
    _, idx = lax.top_k(gate, n_sel)
    valid = idx < own_blk
    b_ix = jnp.arange(B)[:, None, None, None, None]
    h_ix = jnp.arange(KVH)[None, :, None, None, None]
    k_sel = kb[b_ix, h_ix, idx]
    v_sel = vb[b_ix, h_ix, idx]
    s_sel = jnp.einsum('bkgqd,bkgqmnd->bkgqmn', q, k_sel).astype(jnp.float32) * scale
    s_sel = jnp.where(valid[..., None], s_sel, -jnp.inf).reshape(B, KVH, G, Q, n_sel * MOBA_BLOCK)
    s_own = jnp.einsum('bkgqd,bknd->bkgqn', q, k_own).astype(jnp.float32) * scale
    kpos = own_blk * MOBA_BLOCK + jnp.arange(MOBA_BLOCK)
    s_own = jnp.where(kpos[None, :] <= qpos[:, None], s_own, -jnp.inf)
    p = jax.nn.softmax(jnp.concatenate([s_sel, s_own], axis=-1), axis=-1)
    p_sel = p[..., :n_sel * MOBA_BLOCK].reshape(B, KVH, G, Q, n_sel, MOBA_BLOCK).astype(vb.dtype)
    p_own = p[..., n_sel * MOBA_BLOCK:].astype(vb.dtype)
    return (jnp.einsum('bkgqmn,bkgqmnd->bkgqd', p_sel, v_sel)
            + jnp.einsum('bkgqn,bknd->bkgqd', p_own, v_own))


def moba_prompt(q, k, v):
    B, S = q.shape[:2]
    kb, vb = to_blocks(k), to_blocks(v)
    kmean = jnp.mean(kb.astype(jnp.float32), axis=3)
    qh = q.reshape(B, S, N_KV_HEADS, KV_GROUP, HEAD_DIM).transpose(0, 2, 3, 1, 4)
    nq = S // Q_BLOCK
    qc = jnp.moveaxis(qh.reshape(B, N_KV_HEADS, KV_GROUP, nq, Q_BLOCK, HEAD_DIM), 3, 0)

    def step(args):
        qi, c = args
        start = c * Q_BLOCK
        qpos = start + jnp.arange(Q_BLOCK)
        own = start // MOBA_BLOCK
        k_own = lax.dynamic_index_in_dim(kb, own, axis=2, keepdims=False)
        v_own = lax.dynamic_index_in_dim(vb, own, axis=2, keepdims=False)
        return moba_queries(qi, qpos, kb, vb, kmean, k_own, v_own, own)

    out = lax.map(step, (qc, jnp.arange(nq, dtype=jnp.int32)))
    return out.transpose(1, 0, 4, 2, 3, 5).reshape(B, S, ATT_WIDTH)


def moba_sample(q, k_new, v_new, k_past, v_past):
    B, T = q.shape[:2]
    past_len = k_past.shape[1]
    kb = to_blocks(jnp.concatenate([k_past, k_new], axis=1))
    vb = to_blocks(jnp.concatenate([v_past, v_new], axis=1))
    kmean = jnp.mean(kb.astype(jnp.float32), axis=3)
    qh = q.reshape(B, T, N_KV_HEADS, KV_GROUP, HEAD_DIM).transpose(0, 2, 3, 1, 4)
    pos = past_len + jnp.arange(T)

    def one(qi, p):
        own = p // MOBA_BLOCK
        k_own = lax.dynamic_index_in_dim(kb, own, axis=2, keepdims=False)
        v_own = lax.dynamic_index_in_dim(vb, own, axis=2, keepdims=False)
        return moba_queries(qi[:, :, :, None], p[None], kb, vb, kmean, k_own, v_own, own)[:, :, :, 0]

    out = jax.vmap(one, in_axes=(3, 0), out_axes=3)(qh, pos)
    return out.transpose(0, 3, 1, 2, 4).reshape(B, T, ATT_WIDTH)


def gmlp_branch(u_raw, v_raw, g_sgu, w_sp, b_sp):
    B, L, _ = v_raw.shape
    u = jax.nn.gelu(u_raw)
    v = group_layer_norm(jax.nn.gelu(v_raw), g_sgu)
    lp = -(-L // GM_CHUNK) * GM_CHUNK
    up = jnp.pad(u, ((0, 0), (0, lp - L), (0, 0)))
    vp = jnp.pad(v, ((0, 0), (0, lp - L), (0, 0)))
    vc = vp.reshape(B, lp // GM_CHUNK, GM_CHUNK, GM_GROUPS, GM_GROUP_DIM)
    w = jnp.where(jnp.tril(jnp.ones((GM_CHUNK, GM_CHUNK), dtype=bool)), w_sp, 0)
    mixed = jnp.einsum('gts,bcsgd->bctgd', w, vc) + b_sp.T[:, :, None]
    out = up * mixed.reshape(B, lp, GM_WIDTH)
    return out[:, :L], v


def hybrid_layer(x, attend, g_pre, w_in, g_sgu, w_sp, b_sp, w_out, g_post):
    B, L, _ = x.shape
    h = rms_norm(x, g_pre)
    z = h @ w_in
    o1 = ATT_WIDTH
    o2 = o1 + KV_WIDTH
    o3 = o2 + KV_WIDTH
    o4 = o3 + ATT_WIDTH
    o5 = o4 + GM_WIDTH
    o6 = o5 + GM_WIDTH
    q, k, v, ga, u, vs, gg = jnp.split(z, [o1, o2, o3, o4, o5, o6], axis=-1)
    q = q.reshape(B, L, N_HEADS, HEAD_DIM)
    k = k.reshape(B, L, N_KV_HEADS, HEAD_DIM)
    v = v.reshape(B, L, N_KV_HEADS, HEAD_DIM)
    att = attend(q, k, v)
    gm, v_rows = gmlp_branch(u, vs, g_sgu, w_sp, b_sp)
    mix = jnp.concatenate([att * jax.nn.silu(ga), gm * jax.nn.silu(gg)], axis=-1)
    y = x + rms_norm(mix @ w_out, g_post)
    return y, k, v, v_rows


def setup_inputs(seed: int = 0) -> dict:
    key = jax.random.key(seed)
    ks = jax.random.split(key, 14)
    n_pages = PAST_LEN // PAGE_SIZE
    n_used = DEC_BATCH * n_pages
    n_phys = n_used + max(1, n_used // 4)
    nrm = jax.random.normal
    page_table = jax.random.permutation(ks[4], n_phys)[:n_used].reshape(DEC_BATCH, n_pages).astype(jnp.int32)
    return {
        "x_prompt": nrm(ks[0], (BATCH, SEQ, D_MODEL), jnp.float32),
        "x_sample": nrm(ks[1], (DEC_BATCH, DEC_SEQ, D_MODEL), jnp.float32),
        "cache_k": nrm(ks[2], (DEPTH, n_phys, PAGE_SIZE, N_KV_HEADS, HEAD_DIM), jnp.float32),
        "cache_v": nrm(ks[3], (DEPTH, n_phys, PAGE_SIZE, N_KV_HEADS, HEAD_DIM), jnp.float32),
        "page_table": page_table,
        "g_pre": 1.0 + 0.05 * nrm(ks[5], (DEPTH, D_MODEL), jnp.float32),
        "w_in": nrm(ks[6], (DEPTH, D_MODEL, IN_COLS), jnp.float32) * D_MODEL ** -0.5,
        "g_sgu": 1.0 + 0.05 * nrm(ks[7], (DEPTH, GM_WIDTH), jnp.float32),
        "w_spatial": nrm(ks[8], (DEPTH, GM_GROUPS, GM_CHUNK, GM_CHUNK), jnp.float32) * GM_CHUNK ** -0.5,
        "b_spatial": 1.0 + 0.1 * nrm(ks[9], (DEPTH, GM_GROUPS, GM_CHUNK), jnp.float32),
        "w_out": nrm(ks[10], (DEPTH, MIX_WIDTH, D_MODEL), jnp.float32) * MIX_WIDTH ** -0.5,
        "g_post": 1.0 + 0.05 * nrm(ks[11], (DEPTH, D_MODEL), jnp.float32),
    }


def reference(x_prompt, x_sample, cache_k, cache_v, page_table, g_pre, w_in, g_sgu,
              w_spatial, b_spatial, w_out, g_post):
    db, n_pages = page_table.shape
    past_len = n_pages * cache_k.shape[2]
    yp, ys = x_prompt, x_sample
    kp_l, vp_l, ks_l, vs_l, sv_l = [], [], [], [], []
    for l in range(DEPTH):
        yp, kp, vp, _ = hybrid_layer(yp, moba_prompt, g_pre[l], w_in[l], g_sgu[l],
                                     w_spatial[l], b_spatial[l], w_out[l], g_post[l])
        k_past = cache_k[l][page_table].reshape(db, past_len, N_KV_HEADS, HEAD_DIM)
        v_past = cache_v[l][page_table].reshape(db, past_len, N_KV_HEADS, HEAD_DIM)

        def attend_sample(q, k, v, k_past=k_past, v_past=v_past):
            return moba_sample(q, k, v, k_past, v_past)

        ys, kn, vn, sv = hybrid_layer(ys, attend_sample, g_pre[l], w_in[l], g_sgu[l],
                                      w_spatial[l], b_spatial[l], w_out[l], g_post[l])
        kp_l.append(kp)
        vp_l.append(vp)
        ks_l.append(kn)
        vs_l.append(vn)
        sv_l.append(sv)
    new_k_prompt = jnp.stack(kp_l)
    new_v_prompt = jnp.stack(vp_l)
    new_k_sample = jnp.stack(ks_l)
    new_v_sample = jnp.stack(vs_l)
    new_sgu_v_sample = jnp.stack(sv_l)
    return (yp, ys, new_k_prompt, new_v_prompt, new_k_sample, new_v_sample, new_sgu_v_sample)
```

```python
import functools

import numpy as np
import jax
import jax.numpy as jnp
from jax import lax
from jax.experimental import pallas as pl
from jax.experimental.pallas import tpu as pltpu

N_HEADS = 8
HEAD_DIM = 64
N_KV_HEADS = 4
KV_GROUP = N_HEADS // N_KV_HEADS
ATT_WIDTH = N_HEADS * HEAD_DIM
KV_WIDTH = N_KV_HEADS * HEAD_DIM
MOBA_BLOCK = 256
MOBA_TOP_K = 3
GM_GROUPS = 8
GM_GROUP_DIM = 64
GM_WIDTH = GM_GROUPS * GM_GROUP_DIM
GM_CHUNK = 128
NORM_EPS = 1e-6
LN_EPS = 1e-5

LANES = 128
VMEM_LIMIT_BYTES = 56 * 1024 * 1024
PROMPT_ROWS = 512
PAGES_PER_STEP = 8
L_ROWS = 16

F32 = jnp.float32
BF16 = jnp.bfloat16
NEG_INF = float("-inf")


def _dot(a, b):
    return jnp.dot(a, b, preferred_element_type=F32)


def _dot_nt(a, b):
    return lax.dot_general(a, b, (((1,), (1,)), ((), ())), preferred_element_type=F32)


def _gelu_tanh(x):
    c = np.float32(np.sqrt(2.0 / np.pi))
    return x * (0.5 * (1.0 + jnp.tanh(c * (x + np.float32(0.044715) * (x * x * x)))))


def _silu(x):
    return x / (1.0 + jnp.exp(-x))


def _rms_norm(x, g):
    ms = jnp.mean(x * x, axis=-1, keepdims=True)
    return x * lax.rsqrt(ms + NORM_EPS) * g


def _in_proj_kernel(x_ref, gpre_ref, win_ref, gsgu_ref, gavg_ref, wcat_ref, bias_ref, *out_refs, sample):
    if sample:
        q_ref, k_ref, v_ref, sga_ref, gmg_ref, vn_ref = out_refs
    else:
        q_ref, k_ref, v_ref, kb_ref, vt_ref, kmean_ref, sga_ref, gmg_ref = out_refs
    rows = x_ref.shape[0]
    h = _rms_norm(x_ref[...], gpre_ref[...]).astype(BF16)

    def proj(lo, width):
        return _dot(h, win_ref[:, lo:lo + width])

    o_k = ATT_WIDTH
    o_v = o_k + KV_WIDTH
    o_ga = o_v + KV_WIDTH
    o_u = o_ga + ATT_WIDTH
    o_vs = o_u + GM_WIDTH
    o_gg = o_vs + GM_WIDTH

    zq = proj(0, ATT_WIDTH) * np.float32(HEAD_DIM ** -0.5)
    if sample:
        q_ref[...] = zq
    else:
        first_half = lax.broadcasted_iota(jnp.int32, (rows, LANES), 1) < HEAD_DIM
        for kh in range(N_KV_HEADS):
            tile = zq[:, kh * LANES:(kh + 1) * LANES]
            swapped = pltpu.roll(tile, HEAD_DIM, axis=1)
            keep = first_half if kh % 2 == 0 else jnp.logical_not(first_half)
            even_src, odd_src = (tile, swapped) if kh % 2 == 0 else (swapped, tile)
            h0 = 2 * kh
            q_ref[:, h0 * LANES:(h0 + 1) * LANES] = jnp.where(keep, even_src, 0.0).astype(BF16)
            q_ref[:, (h0 + 1) * LANES:(h0 + 2) * LANES] = jnp.where(keep, odd_src, 0.0).astype(BF16)

    zk = proj(o_k, KV_WIDTH)
    k_ref[...] = zk
    zv = proj(o_v, KV_WIDTH)
    v_ref[...] = zv
    if not sample:
        kb_ref[...] = zk.astype(BF16)
        for c in range(rows // MOBA_BLOCK):
            blk = slice(c * MOBA_BLOCK, (c + 1) * MOBA_BLOCK)
            kmean_ref[c] = jnp.sum(zk[blk], axis=0, keepdims=True) * np.float32(1.0 / MOBA_BLOCK)
            vt_ref[c] = zv[blk].T.astype(BF16)

    sga_ref[...] = _silu(proj(o_ga, ATT_WIDTH)).astype(BF16)

    u = _gelu_tanh(proj(o_u, GM_WIDTH))
    gv = _gelu_tanh(proj(o_vs, GM_WIDTH))
    mu = _dot(gv.astype(BF16), gavg_ref[...])
    d = gv - mu
    var = _dot((d * d).astype(BF16), gavg_ref[...])
    vn = d * lax.rsqrt(var + LN_EPS) * gsgu_ref[...]
    if sample:
        vn_ref[...] = vn

    r_i = lax.broadcasted_iota(jnp.int32, (GM_CHUNK, 2 * GM_CHUNK), 0)
    c_i = lax.broadcasted_iota(jnp.int32, (GM_CHUNK, 2 * GM_CHUNK), 1)
    tril = r_i >= (c_i & (GM_CHUNK - 1))
    lane_lo = lax.broadcasted_iota(jnp.int32, (GM_CHUNK, LANES), 1) < GM_GROUP_DIM
    sgg = _silu(proj(o_gg, GM_WIDTH))
    for p in range(GM_GROUPS // 2):
        w_pair = jnp.where(tril, wcat_ref[p], 0.0).astype(BF16)
        cols = slice(p * LANES, (p + 1) * LANES)
        for c in range(rows // GM_CHUNK):
            rws = slice(c * GM_CHUNK, (c + 1) * GM_CHUNK)
            vp = vn[rws, cols]
            rhs = jnp.concatenate([jnp.where(lane_lo, vp, 0.0).astype(BF16),
                                   jnp.where(lane_lo, 0.0, vp).astype(BF16)], axis=0)
            mixed = _dot(w_pair, rhs) + bias_ref[:, cols]
            gmg_ref[rws, cols] = (u[rws, cols] * mixed * sgg[rws, cols]).astype(BF16)


def _in_proj(x, g_pre, w_in16, g_sgu, gavg, wcat, bias, *, rows, sample):
    n, d_model = x.shape
    assert n % rows == 0 and rows % GM_CHUNK == 0
    steps = n // rows
    row_spec = lambda width: pl.BlockSpec((rows, width), lambda i: (i, 0))
    full = lambda a: pl.BlockSpec(a.shape, lambda i: (0,) * a.ndim)
    if sample:
        out_shape = [jax.ShapeDtypeStruct((n, ATT_WIDTH), F32),
                     jax.ShapeDtypeStruct((n, KV_WIDTH), F32),
                     jax.ShapeDtypeStruct((n, KV_WIDTH), F32),
                     jax.ShapeDtypeStruct((n, ATT_WIDTH), BF16),
                     jax.ShapeDtypeStruct((n, GM_WIDTH), BF16),
                     jax.ShapeDtypeStruct((n, GM_WIDTH), F32)]
        out_specs = [row_spec(ATT_WIDTH), row_spec(KV_WIDTH), row_spec(KV_WIDTH),
                     row_spec(ATT_WIDTH), row_spec(GM_WIDTH), row_spec(GM_WIDTH)]
    else:
        assert rows % MOBA_BLOCK == 0
        bps = rows // MOBA_BLOCK
        nblk = n // MOBA_BLOCK
        out_shape = [jax.ShapeDtypeStruct((n, N_HEADS * LANES), BF16),
                     jax.ShapeDtypeStruct((n, KV_WIDTH), F32),
                     jax.ShapeDtypeStruct((n, KV_WIDTH), F32),
                     jax.ShapeDtypeStruct((n, KV_WIDTH), BF16),
                     jax.ShapeDtypeStruct((nblk, KV_WIDTH, MOBA_BLOCK), BF16),
                     jax.ShapeDtypeStruct((nblk, 1, KV_WIDTH), F32),
                     jax.ShapeDtypeStruct((n, ATT_WIDTH), BF16),
                     jax.ShapeDtypeStruct((n, GM_WIDTH), BF16)]
        out_specs = [row_spec(N_HEADS * LANES), row_spec(KV_WIDTH), row_spec(KV_WIDTH), row_spec(KV_WIDTH),
                     pl.BlockSpec((bps, KV_WIDTH, MOBA_BLOCK), lambda i: (i, 0, 0)),
                     pl.BlockSpec((bps, 1, KV_WIDTH), lambda i: (i, 0, 0)),
                     row_spec(ATT_WIDTH), row_spec(GM_WIDTH)]
    return pl.pallas_call(
        functools.partial(_in_proj_kernel, sample=sample),
        out_shape=out_shape,
        grid=(steps,),
        in_specs=[row_spec(d_model), full(g_pre), full(w_in16), full(g_sgu), full(gavg), full(wcat), full(bias)],
        out_specs=out_specs,
        compiler_params=pltpu.CompilerParams(dimension_semantics=("arbitrary",),
                                             vmem_limit_bytes=VMEM_LIMIT_BYTES),
        name="in_proj_sample" if sample else "in_proj_prompt",
    )(x, g_pre, w_in16, g_sgu, gavg, wcat, bias)


def _tail(att, sga_ref, gmg_ref, x_ref, wout_ref, gpost_ref):
    mix = jnp.concatenate([(att * sga_ref[...].astype(F32)).astype(BF16), gmg_ref[...]], axis=-1)
    out = _dot(mix, wout_ref[...])
    return x_ref[...] + _rms_norm(out, gpost_ref[...])


def _prompt_attn_kernel(q_ref, kb_ref, vt_ref, kmean_ref, sga_ref, gmg_ref, x_ref, wout_ref, gpost_ref,
                        y_ref, sel_ref, m_ref, acc_ref, attt_ref):
    i = pl.program_id(1)
    nblk = kb_ref.shape[0]
    tq = q_ref.shape[0]
    ones_rows = jnp.ones((L_ROWS, MOBA_BLOCK), BF16)

    def q_head(h):
        return q_ref[:, h * LANES:(h + 1) * LANES]

    def k_tile(j, h):
        a = (h // KV_GROUP) // 2
        return kb_ref[j, :, a * LANES:(a + 1) * LANES]

    def vt_aug(j, h):
        kh = h // KV_GROUP
        return jnp.concatenate([vt_ref[j, kh * HEAD_DIM:(kh + 1) * HEAD_DIM, :], ones_rows], axis=0)

    blk_i = lax.broadcasted_iota(jnp.int32, (nblk, tq), 0)
    km = kmean_ref[...]
    for h in range(N_HEADS):
        a = (h // KV_GROUP) // 2
        km_a = km[:, a * LANES:(a + 1) * LANES]
        km_hi = km_a.astype(BF16)
        km_lo = (km_a - km_hi.astype(F32)).astype(BF16)
        g2 = _dot_nt(jnp.concatenate([km_hi, km_lo], axis=0), q_head(h))
        g = g2[:nblk] + g2[nblk:]
        rank = jnp.zeros((nblk, tq), jnp.int32)
        for n2 in range(nblk):
            gb = g[n2:n2 + 1, :]
            beats = (gb > g) | ((gb == g) & (blk_i > n2))
            rank = rank + jnp.where(beats, 1, 0) * (n2 < i).astype(jnp.int32)
        sel_ref[h] = jnp.where((blk_i < i) & (rank < MOBA_TOP_K), 1.0, 0.0)

    causal = (lax.broadcasted_iota(jnp.int32, (MOBA_BLOCK, tq), 0)
              <= lax.broadcasted_iota(jnp.int32, (MOBA_BLOCK, tq), 1))
    for h in range(N_HEADS):
        s = jnp.where(causal, _dot_nt(k_tile(i, h), q_head(h)), NEG_INF)
        m = jnp.max(s, axis=0, keepdims=True)
        p = jnp.exp(s - m).astype(BF16)
        m_ref[h] = m
        acc_ref[h] = _dot(vt_aug(i, h), p)

    def past_block(j, carry):
        for h in range(N_HEADS):
            s = _dot_nt(k_tile(j, h), q_head(h))
            chosen = sel_ref[h, pl.ds(j, 1), :] > 0.5
            m_old = m_ref[h]
            m_new = jnp.where(chosen, jnp.maximum(m_old, jnp.max(s, axis=0, keepdims=True)), m_old)
            p = jnp.exp(s - jnp.where(chosen, m_new, jnp.inf)).astype(BF16)
            acc_ref[h] = acc_ref[h] * jnp.exp(m_old - m_new) + _dot(vt_aug(j, h), p)
            m_ref[h] = m_new
        return carry

    lax.fori_loop(0, i, past_block, 0)

    for h in range(N_HEADS):
        acc = acc_ref[h]
        attt_ref[h * HEAD_DIM:(h + 1) * HEAD_DIM, :] = acc[:HEAD_DIM] / acc[HEAD_DIM:HEAD_DIM + 1]
    y_ref[...] = _tail(attt_ref[...].T, sga_ref, gmg_ref, x_ref, wout_ref, gpost_ref)


def _prompt_attn(qpad, kb, vt, kmean, sga, gmg, x, w_out16, g_post, *, batch, seq):
    nblk = seq // MOBA_BLOCK
    d_model = x.shape[-1]
    tile = lambda width: pl.BlockSpec((MOBA_BLOCK, width), lambda b, i: (b * nblk + i, 0))
    per_batch = lambda r, c: pl.BlockSpec((None, nblk, r, c), lambda b, i: (b, 0, 0, 0))
    full = lambda a: pl.BlockSpec(a.shape, lambda b, i: (0,) * a.ndim)
    return pl.pallas_call(
        _prompt_attn_kernel,
        out_shape=jax.ShapeDtypeStruct(x.shape, F32),
        grid=(batch, nblk),
        in_specs=[tile(N_HEADS * LANES),
                  per_batch(MOBA_BLOCK, KV_WIDTH),
                  per_batch(KV_WIDTH, MOBA_BLOCK),
                  pl.BlockSpec((None, nblk, KV_WIDTH), lambda b, i: (b, 0, 0)),
                  tile(ATT_WIDTH), tile(GM_WIDTH), tile(d_model), full(w_out16), full(g_post)],
        out_specs=tile(d_model),
        scratch_shapes=[pltpu.VMEM((N_HEADS, nblk, MOBA_BLOCK), F32),
                        pltpu.VMEM((N_HEADS, 1, MOBA_BLOCK), F32),
                        pltpu.VMEM((N_HEADS, HEAD_DIM + L_ROWS, MOBA_BLOCK), F32),
                        pltpu.VMEM((ATT_WIDTH, MOBA_BLOCK), F32)],
        compiler_params=pltpu.CompilerParams(dimension_semantics=("arbitrary", "arbitrary"),
                                             vmem_limit_bytes=VMEM_LIMIT_BYTES),
        name="prompt_attn",
    )(qpad, kb.reshape(batch, nblk, MOBA_BLOCK, KV_WIDTH), vt.reshape(batch, nblk, KV_WIDTH, MOBA_BLOCK),
      kmean.reshape(batch, nblk, KV_WIDTH), sga, gmg, x, w_out16, g_post)


def _sample_attn_kernel(pt_ref, qbd_ref, knew_ref, vnew_ref, *refs, pages_per_step, n_past_blocks, page_size):
    k_pages = refs[:pages_per_step]
    v_pages = refs[pages_per_step:2 * pages_per_step]
    out_ref, o_ref, m_ref, l_ref, g_ref = refs[2 * pages_per_step:]
    s_idx = pl.program_id(1)
    n_steps = pl.num_programs(1)
    pages_per_block = MOBA_BLOCK // page_size
    blocks_per_step = pages_per_step // pages_per_block
    nq = qbd_ref.shape[0]
    t_new = knew_ref.shape[0]
    lane = lax.broadcasted_iota(jnp.int32, (nq, LANES), 1)

    @pl.when(s_idx == 0)
    def _():
        m_ref[...] = jnp.full((nq, LANES), NEG_INF, F32)
        l_ref[...] = jnp.zeros((nq, LANES), F32)
        g_ref[...] = jnp.full((nq, LANES), NEG_INF, F32)

    qbd = qbd_ref[...]
    qbd16 = qbd.astype(BF16)
    for jj in range(blocks_per_step):
        pages = range(jj * pages_per_block, (jj + 1) * pages_per_block)
        kblk = jnp.concatenate([k_pages[r][...] for r in pages], axis=0)
        vblk = jnp.concatenate([v_pages[r][...] for r in pages], axis=0)
        blk = s_idx * blocks_per_step + jj
        kmean = jnp.sum(kblk, axis=0, keepdims=True) * np.float32(1.0 / MOBA_BLOCK)
        gate = jnp.sum(qbd * kmean, axis=-1, keepdims=True)
        s = _dot_nt(qbd16, kblk.astype(BF16))
        m = jnp.max(s, axis=-1, keepdims=True)
        p = jnp.exp(s - m)
        o_ref[blk] = _dot(p.astype(BF16), vblk.astype(BF16))
        hit = lane == blk
        m_ref[...] = jnp.where(hit, m, m_ref[...])
        l_ref[...] = jnp.where(hit, jnp.sum(p, axis=-1, keepdims=True), l_ref[...])
        g_ref[...] = jnp.where(hit, gate, g_ref[...])

    @pl.when(s_idx == n_steps - 1)
    def _():
        g = g_ref[...]
        lane_f = lane.astype(F32)
        sel = jnp.zeros((nq, LANES), jnp.bool_)
        for _ in range(min(MOBA_TOP_K, n_past_blocks)):
            best = jnp.max(g, axis=-1, keepdims=True)
            idx = jnp.min(jnp.where(g == best, lane_f, np.float32(LANES)), axis=-1, keepdims=True)
            pick = lane_f == idx
            sel = sel | pick
            g = jnp.where(pick, NEG_INF, g)
        row = lax.broadcasted_iota(jnp.int32, (nq, LANES), 0)
        rows_per_token = nq // t_new
        s_own = jnp.full((nq, LANES), NEG_INF, F32)
        for t in range(t_new):
            s_t = jnp.sum(qbd * knew_ref[t:t + 1, :], axis=-1, keepdims=True)
            s_own = jnp.where((lane == t) & (row >= t * rows_per_token), s_t, s_own)
        m_blocks = m_ref[...]
        m_tot = jnp.maximum(jnp.max(jnp.where(sel, m_blocks, NEG_INF), axis=-1, keepdims=True),
                            jnp.max(s_own, axis=-1, keepdims=True))
        w = jnp.where(sel, jnp.exp(m_blocks - m_tot), 0.0)
        p_own = jnp.exp(s_own - m_tot)
        l_tot = (jnp.sum(w * l_ref[...], axis=-1, keepdims=True) + jnp.sum(p_own, axis=-1, keepdims=True))
        acc = jnp.zeros((nq, KV_WIDTH), F32)
        for n in range(n_past_blocks):
            acc = acc + jnp.sum(jnp.where(lane == n, w, 0.0), axis=-1, keepdims=True) * o_ref[n]
        for t in range(t_new):
            acc = acc + jnp.sum(jnp.where(lane == t, p_own, 0.0), axis=-1, keepdims=True) * vnew_ref[t:t + 1, :]
        out_ref[...] = acc / l_tot


def _sample_attn(page_table, qbd, k_new, v_new, cache_k, cache_v):
    db, n_pages = page_table.shape
    page_size = cache_k.shape[1]
    nq = qbd.shape[1]
    t_new = k_new.shape[1]
    pps = min(PAGES_PER_STEP, n_pages)
    pages_per_block = MOBA_BLOCK // page_size
    assert n_pages % pps == 0 and pps % pages_per_block == 0
    n_past_blocks = n_pages // pages_per_block
    assert n_past_blocks <= LANES and t_new <= LANES

    def page_spec(r):
        return pl.BlockSpec((None, page_size, KV_WIDTH), lambda b, s, pt: (pt[b, s * pps + r], 0, 0))

    per_seq = lambda rows: pl.BlockSpec((None, rows, KV_WIDTH), lambda b, s, pt: (b, 0, 0))
    kernel = functools.partial(_sample_attn_kernel, pages_per_step=pps, n_past_blocks=n_past_blocks,
                               page_size=page_size)
    return pl.pallas_call(
        kernel,
        out_shape=jax.ShapeDtypeStruct((db, nq, KV_WIDTH), F32),
        grid_spec=pltpu.PrefetchScalarGridSpec(
            num_scalar_prefetch=1,
            grid=(db, n_pages // pps),
            in_specs=([per_seq(nq), per_seq(t_new), per_seq(t_new)]
                      + [page_spec(r) for r in range(pps)] + [page_spec(r) for r in range(pps)]),
            out_specs=per_seq(nq),
            scratch_shapes=[pltpu.VMEM((n_past_blocks, nq, KV_WIDTH), F32),
                            pltpu.VMEM((nq, LANES), F32),
                            pltpu.VMEM((nq, LANES), F32),
                            pltpu.VMEM((nq, LANES), F32)]),
        compiler_params=pltpu.CompilerParams(dimension_semantics=("arbitrary", "arbitrary"),
                                             vmem_limit_bytes=VMEM_LIMIT_BYTES),
        name="sample_attn",
    )(page_table, qbd, k_new, v_new, *([cache_k] * pps), *([cache_v] * pps))


def _sample_tail_kernel(att_ref, sga_ref, gmg_ref, x_ref, wout_ref, gpost_ref, y_ref):
    y_ref[...] = _tail(att_ref[...], sga_ref, gmg_ref, x_ref, wout_ref, gpost_ref)


def _sample_tail(att, sga, gmg, x, w_out16, g_post):
    return pl.pallas_call(
        _sample_tail_kernel,
        out_shape=jax.ShapeDtypeStruct(x.shape, F32),
        compiler_params=pltpu.CompilerParams(vmem_limit_bytes=VMEM_LIMIT_BYTES),
        name="sample_tail",
    )(att, sga, gmg, x, w_out16, g_post)


def _pair_cat(w):
    g, c, _ = w.shape
    return w.reshape(g // 2, 2, c, c).transpose(0, 2, 1, 3).reshape(g // 2, c, 2 * c)


def _layer(x_p, x_s, cache_k, cache_v, page_table, g_pre, w_in, g_sgu, w_sp, b_sp, w_out, g_post):
    batch, seq, d_model = x_p.shape
    db, t_new, _ = x_s.shape
    assert seq % MOBA_BLOCK == 0 and GM_CHUNK % t_new == 0
    w_in16 = w_in.astype(BF16)
    w_out16 = w_out.astype(BF16)
    g_pre2, g_sgu2, g_post2 = g_pre[None, :], g_sgu[None, :], g_post[None, :]
    grp = jnp.arange(GM_WIDTH) // GM_GROUP_DIM
    gavg = jnp.where(grp[:, None] == grp[None, :], 1.0 / GM_GROUP_DIM, 0.0).astype(BF16)
    bias_p = jnp.repeat(b_sp.T, GM_GROUP_DIM, axis=1)
    wcat_p = _pair_cat(w_sp)
    seqs = GM_CHUNK // t_new
    w_s = jnp.einsum('ab,gts->gatbs', jnp.eye(seqs, dtype=F32), w_sp[:, :t_new, :t_new])
    wcat_s = _pair_cat(w_s.reshape(GM_GROUPS, GM_CHUNK, GM_CHUNK))
    bias_s = jnp.tile(bias_p[:t_new], (seqs, 1))

    xp2 = x_p.reshape(batch * seq, d_model)
    qpad, k_p, v_p, kb, vt, kmean, sga_p, gmg_p = _in_proj(
        xp2, g_pre2, w_in16, g_sgu2, gavg, wcat_p, bias_p, rows=PROMPT_ROWS, sample=False)
    y_p = _prompt_attn(qpad, kb, vt, kmean, sga_p, gmg_p, xp2, w_out16, g_post2, batch=batch, seq=seq)

    xs2 = x_s.reshape(db * t_new, d_model)
    q_s, k_s, v_s, sga_s, gmg_s, vn_s = _in_proj(
        xs2, g_pre2, w_in16, g_sgu2, gavg, wcat_s, bias_s, rows=db * t_new, sample=True)
    q5 = q_s.reshape(db, t_new, N_KV_HEADS, KV_GROUP, HEAD_DIM)
    eye = jnp.eye(N_KV_HEADS, dtype=F32)[None, None, :, None, :, None]
    qbd = (q5[:, :, :, :, None, :] * eye).reshape(db, t_new * N_HEADS, KV_WIDTH)
    ck = cache_k.reshape(cache_k.shape[0], cache_k.shape[1], KV_WIDTH)
    cv = cache_v.reshape(cache_v.shape[0], cache_v.shape[1], KV_WIDTH)
    o_full = _sample_attn(page_table, qbd, k_s.reshape(db, t_new, KV_WIDTH), v_s.reshape(db, t_new, KV_WIDTH), ck, cv)
    o6 = o_full.reshape(db, t_new, N_KV_HEADS, KV_GROUP, N_KV_HEADS, HEAD_DIM)
    att_s = jnp.stack([o6[:, :, kh, :, kh, :] for kh in range(N_KV_HEADS)], axis=2)
    att_s = att_s.reshape(db * t_new, ATT_WIDTH)
    y_s = _sample_tail(att_s, sga_s, gmg_s, xs2, w_out16, g_post2)

    kv_p = (batch, seq, N_KV_HEADS, HEAD_DIM)
    kv_s = (db, t_new, N_KV_HEADS, HEAD_DIM)
    return (y_p.reshape(batch, seq, d_model), y_s.reshape(db, t_new, d_model),
            k_p.reshape(kv_p), v_p.reshape(kv_p), k_s.reshape(kv_s), v_s.reshape(kv_s),
            vn_s.reshape(db, t_new, GM_WIDTH))


def kernel(x_prompt, x_sample, cache_k, cache_v, page_table, g_pre, w_in, g_sgu, w_spatial, b_spatial, w_out, g_post):
    yp, ys = x_prompt, x_sample
    outs = []
    for l in range(w_in.shape[0]):
        yp, ys, kp, vp, kn, vn, sv = _layer(yp, ys, cache_k[l], cache_v[l], page_table, g_pre[l], w_in[l],
                                            g_sgu[l], w_spatial[l], b_spatial[l], w_out[l], g_post[l])
        outs.append((kp, vp, kn, vn, sv))
    stacked = [jnp.stack(parts) for parts in zip(*outs)]
    return (yp, ys, *stacked)
```

```python
import functools

import numpy as np
import jax
import jax.numpy as jnp
from jax import lax
from jax.experimental import pallas as pl
from jax.experimental.pallas import tpu as pltpu

N_HEADS = 8
HEAD_DIM = 64
N_KV_HEADS = 4
KV_GROUP = N_HEADS // N_KV_HEADS
ATT_WIDTH = N_HEADS * HEAD_DIM
KV_WIDTH = N_KV_HEADS * HEAD_DIM
MOBA_BLOCK = 256
MOBA_TOP_K = 3
GM_GROUPS = 8
GM_GROUP_DIM = 64
GM_WIDTH = GM_GROUPS * GM_GROUP_DIM
GM_CHUNK = 128
NORM_EPS = 1e-6
LN_EPS = 1e-5

LANES = 128
VMEM_LIMIT_BYTES = 56 * 1024 * 1024
PROMPT_ROWS = 512
PAGES_PER_STEP = 8
L_ROWS = 16

F32 = jnp.float32
BF16 = jnp.bfloat16
NEG_INF = float("-inf")
Q_SCALE = np.float32(HEAD_DIM ** -0.5 * np.log2(np.e))


def _dot(a, b):
    return jnp.dot(a, b, preferred_element_type=F32)


def _dot_nt(a, b):
    return lax.dot_general(a, b, (((1,), (1,)), ((), ())), preferred_element_type=F32)


def _gelu_tanh(x):
    c = np.float32(np.sqrt(2.0 / np.pi))
    return x * (0.5 * (1.0 + jnp.tanh(c * (x + np.float32(0.044715) * (x * x * x)))))


def _silu(x):
    return x / (1.0 + jnp.exp(-x))


def _rms_norm(x, g):
    ms = jnp.mean(x * x, axis=-1, keepdims=True)
    return x * lax.rsqrt(ms + NORM_EPS) * g


def _in_proj_kernel(x_ref, gpre_ref, win_ref, gsgu_ref, gavg_ref, wcat_ref, bias_ref, *out_refs, sample):
    if sample:
        q_ref, k_ref, v_ref, sga_ref, gmg_ref, vn_ref = out_refs
    else:
        q_ref, k_ref, v_ref, kb_ref, vt_ref, kmean_ref, sga_ref, gmg_ref = out_refs
    rows = x_ref.shape[0]
    h = _rms_norm(x_ref[...], gpre_ref[...]).astype(BF16)

    def proj(lo, width):
        return _dot(h, win_ref[:, lo:lo + width])

    o_k = ATT_WIDTH
    o_v = o_k + KV_WIDTH
    o_ga = o_v + KV_WIDTH
    o_u = o_ga + ATT_WIDTH
    o_vs = o_u + GM_WIDTH
    o_gg = o_vs + GM_WIDTH

    zq = proj(0, ATT_WIDTH) * Q_SCALE
    if sample:
        q_ref[...] = zq
    else:
        first_half = lax.broadcasted_iota(jnp.int32, (rows, LANES), 1) < HEAD_DIM
        for kh in range(N_KV_HEADS):
            tile = zq[:, kh * LANES:(kh + 1) * LANES]
            swapped = pltpu.roll(tile, HEAD_DIM, axis=1)
            keep = first_half if kh % 2 == 0 else jnp.logical_not(first_half)
            even_src, odd_src = (tile, swapped) if kh % 2 == 0 else (swapped, tile)
            h0 = 2 * kh
            q_ref[:, h0 * LANES:(h0 + 1) * LANES] = jnp.where(keep, even_src, 0.0).astype(BF16)
            q_ref[:, (h0 + 1) * LANES:(h0 + 2) * LANES] = jnp.where(keep, odd_src, 0.0).astype(BF16)

    zk = proj(o_k, KV_WIDTH)
    zv = proj(o_v, KV_WIDTH)
    if sample:
        k_ref[...] = zk
        v_ref[...] = zv
    else:
        k_ref[...] = zk.T
        zvt = zv.T
        v_ref[...] = zvt
        kb_ref[...] = zk.astype(BF16)
        for c in range(rows // MOBA_BLOCK):
            blk = slice(c * MOBA_BLOCK, (c + 1) * MOBA_BLOCK)
            kmean_ref[c] = jnp.sum(zk[blk], axis=0, keepdims=True) * np.float32(1.0 / MOBA_BLOCK)
            vt_ref[c] = zvt[:, blk].astype(BF16)

    sga_ref[...] = _silu(proj(o_ga, ATT_WIDTH)).astype(BF16)

    u = _gelu_tanh(proj(o_u, GM_WIDTH))
    gv = _gelu_tanh(proj(o_vs, GM_WIDTH))
    mu = _dot(gv.astype(BF16), gavg_ref[...])
    d = gv - mu
    var = _dot((d * d).astype(BF16), gavg_ref[...])
    vn = d * lax.rsqrt(var + LN_EPS) * gsgu_ref[...]
    if sample:
        vn_ref[...] = vn

    r_i = lax.broadcasted_iota(jnp.int32, (GM_CHUNK, 2 * GM_CHUNK), 0)
    c_i = lax.broadcasted_iota(jnp.int32, (GM_CHUNK, 2 * GM_CHUNK), 1)
    tril = r_i >= (c_i & (GM_CHUNK - 1))
    lane_lo = lax.broadcasted_iota(jnp.int32, (GM_CHUNK, LANES), 1) < GM_GROUP_DIM
    sgg = _silu(proj(o_gg, GM_WIDTH))
    for p in range(GM_GROUPS // 2):
        w_pair = jnp.where(tril, wcat_ref[p], 0.0).astype(BF16)
        cols = slice(p * LANES, (p + 1) * LANES)
        for c in range(rows // GM_CHUNK):
            rws = slice(c * GM_CHUNK, (c + 1) * GM_CHUNK)
            vp = vn[rws, cols]
            rhs = jnp.concatenate([jnp.where(lane_lo, vp, 0.0).astype(BF16),
                                   jnp.where(lane_lo, 0.0, vp).astype(BF16)], axis=0)
            mixed = _dot(w_pair, rhs) + bias_ref[:, cols]
            gmg_ref[rws, cols] = (u[rws, cols] * mixed * sgg[rws, cols]).astype(BF16)


def _in_proj(x, g_pre, w_in16, g_sgu, gavg, wcat, bias, *, rows, sample, seq=None):
    n, d_model = x.shape
    assert n % rows == 0 and rows % GM_CHUNK == 0
    steps = n // rows
    row_spec = lambda width: pl.BlockSpec((rows, width), lambda i: (i, 0))
    full = lambda a: pl.BlockSpec(a.shape, lambda i: (0,) * a.ndim)
    if sample:
        out_shape = [jax.ShapeDtypeStruct((n, ATT_WIDTH), F32),
                     jax.ShapeDtypeStruct((n, KV_WIDTH), F32),
                     jax.ShapeDtypeStruct((n, KV_WIDTH), F32),
                     jax.ShapeDtypeStruct((n, ATT_WIDTH), BF16),
                     jax.ShapeDtypeStruct((n, GM_WIDTH), BF16),
                     jax.ShapeDtypeStruct((n, GM_WIDTH), F32)]
        out_specs = [row_spec(ATT_WIDTH), row_spec(KV_WIDTH), row_spec(KV_WIDTH),
                     row_spec(ATT_WIDTH), row_spec(GM_WIDTH), row_spec(GM_WIDTH)]
    else:
        assert rows % MOBA_BLOCK == 0 and seq % rows == 0
        bps = rows // MOBA_BLOCK
        nblk = n // MOBA_BLOCK
        spb = seq // rows
        kv_t_spec = pl.BlockSpec((None, KV_WIDTH, rows), lambda i: (i // spb, 0, i % spb))
        out_shape = [jax.ShapeDtypeStruct((n, N_HEADS * LANES), BF16),
                     jax.ShapeDtypeStruct((n // seq, KV_WIDTH, seq), F32),
                     jax.ShapeDtypeStruct((n // seq, KV_WIDTH, seq), F32),
                     jax.ShapeDtypeStruct((n, KV_WIDTH), BF16),
                     jax.ShapeDtypeStruct((nblk, KV_WIDTH, MOBA_BLOCK), BF16),
                     jax.ShapeDtypeStruct((nblk, 1, KV_WIDTH), F32),
                     jax.ShapeDtypeStruct((n, ATT_WIDTH), BF16),
                     jax.ShapeDtypeStruct((n, GM_WIDTH), BF16)]
        out_specs = [row_spec(N_HEADS * LANES), kv_t_spec, kv_t_spec, row_spec(KV_WIDTH),
                     pl.BlockSpec((bps, KV_WIDTH, MOBA_BLOCK), lambda i: (i, 0, 0)),
                     pl.BlockSpec((bps, 1, KV_WIDTH), lambda i: (i, 0, 0)),
                     row_spec(ATT_WIDTH), row_spec(GM_WIDTH)]
    return pl.pallas_call(
        functools.partial(_in_proj_kernel, sample=sample),
        out_shape=out_shape,
        grid=(steps,),
        in_specs=[row_spec(d_model), full(g_pre), full(w_in16), full(g_sgu), full(gavg), full(wcat), full(bias)],
        out_specs=out_specs,
        compiler_params=pltpu.CompilerParams(dimension_semantics=("arbitrary",),
                                             vmem_limit_bytes=VMEM_LIMIT_BYTES),
        name="in_proj_sample" if sample else "in_proj_prompt",
    )(x, g_pre, w_in16, g_sgu, gavg, wcat, bias)


def _tail(att, sga_ref, gmg_ref, x_ref, wout_ref, gpost_ref):
    mix = jnp.concatenate([(att * sga_ref[...].astype(F32)).astype(BF16), gmg_ref[...]], axis=-1)
    out = _dot(mix, wout_ref[...])
    return x_ref[...] + _rms_norm(out, gpost_ref[...])


def _prompt_attn_kernel(q_ref, kb_ref, vt_ref, kmean_ref, sga_ref, gmg_ref, x_ref, wout_ref, gpost_ref,
                        y_ref, sel_ref, m_ref, acc_ref, attt_ref, s_ref, bm_ref):
    i = pl.program_id(1)
    nblk = kb_ref.shape[0]
    tq = q_ref.shape[0]
    ones_rows = jnp.ones((L_ROWS, MOBA_BLOCK), BF16)

    def q_head(h):
        return q_ref[:, h * LANES:(h + 1) * LANES]

    def k_tile(j, h):
        a = (h // KV_GROUP) // 2
        return kb_ref[j, :, a * LANES:(a + 1) * LANES]

    def vt_aug(j, h):
        kh = h // KV_GROUP
        return jnp.concatenate([vt_ref[j, kh * HEAD_DIM:(kh + 1) * HEAD_DIM, :], ones_rows], axis=0)

    blk_i = lax.broadcasted_iota(jnp.int32, (nblk, tq), 0)
    km = kmean_ref[...]
    for h in range(N_HEADS):
        a = (h // KV_GROUP) // 2
        km_a = km[:, a * LANES:(a + 1) * LANES]
        km_hi = km_a.astype(BF16)
        km_lo = (km_a - km_hi.astype(F32)).astype(BF16)
        g2 = _dot_nt(jnp.concatenate([km_hi, km_lo], axis=0), q_head(h))
        g = g2[:nblk] + g2[nblk:]
        rank = jnp.zeros((nblk, tq), jnp.int32)
        for n2 in range(nblk):
            gb = g[n2:n2 + 1, :]
            beats = (gb > g) | ((gb == g) & (blk_i > n2))
            rank = rank + jnp.where(beats, 1, 0) * (n2 < i).astype(jnp.int32)
        sel_ref[h] = jnp.where((blk_i < i) & (rank < MOBA_TOP_K), 1.0, 0.0)

    causal = (lax.broadcasted_iota(jnp.int32, (MOBA_BLOCK, tq), 0)
              <= lax.broadcasted_iota(jnp.int32, (MOBA_BLOCK, tq), 1))
    def scores(j, mask):
        for h in range(N_HEADS):
            s = _dot_nt(k_tile(j, h), q_head(h))
            if mask is not None:
                s = jnp.where(mask, s, NEG_INF)
            s_ref[h] = s
            bm_ref[h] = jnp.max(s, axis=0, keepdims=True)

    scores(i, causal)
    for h in range(N_HEADS):
        m = bm_ref[h]
        p = jnp.exp2(s_ref[h] - m).astype(BF16)
        m_ref[h] = m
        acc_ref[h] = _dot(vt_aug(i, h), p)

    def past_block(j, carry):
        scores(j, None)
        for h in range(N_HEADS):
            chosen = sel_ref[h, pl.ds(j, 1), :] > 0.5
            m_old = m_ref[h]
            m_new = jnp.where(chosen, jnp.maximum(m_old, bm_ref[h]), m_old)
            p = jnp.exp2(s_ref[h] - jnp.where(chosen, m_new, jnp.inf)).astype(BF16)
            acc_ref[h] = acc_ref[h] * jnp.exp2(m_old - m_new) + _dot(vt_aug(j, h), p)
            m_ref[h] = m_new
        return carry

    lax.fori_loop(0, i, past_block, 0)

    for h in range(N_HEADS):
        acc = acc_ref[h]
        attt_ref[h * HEAD_DIM:(h + 1) * HEAD_DIM, :] = acc[:HEAD_DIM] / acc[HEAD_DIM:HEAD_DIM + 1]
    y_ref[...] = _tail(attt_ref[...].T, sga_ref, gmg_ref, x_ref, wout_ref, gpost_ref)


def _prompt_attn(qpad, kb, vt, kmean, sga, gmg, x, w_out16, g_post, *, batch, seq):
    nblk = seq // MOBA_BLOCK
    d_model = x.shape[-1]
    tile = lambda width: pl.BlockSpec((MOBA_BLOCK, width), lambda b, i: (b * nblk + i, 0))
    per_batch = lambda r, c: pl.BlockSpec((None, nblk, r, c), lambda b, i: (b, 0, 0, 0))
    full = lambda a: pl.BlockSpec(a.shape, lambda b, i: (0,) * a.ndim)
    return pl.pallas_call(
        _prompt_attn_kernel,
        out_shape=jax.ShapeDtypeStruct(x.shape, F32),
        grid=(batch, nblk),
        in_specs=[tile(N_HEADS * LANES),
                  per_batch(MOBA_BLOCK, KV_WIDTH),
                  per_batch(KV_WIDTH, MOBA_BLOCK),
                  pl.BlockSpec((None, nblk, KV_WIDTH), lambda b, i: (b, 0, 0)),
                  tile(ATT_WIDTH), tile(GM_WIDTH), tile(d_model), full(w_out16), full(g_post)],
        out_specs=tile(d_model),
        scratch_shapes=[pltpu.VMEM((N_HEADS, nblk, MOBA_BLOCK), F32),
                        pltpu.VMEM((N_HEADS, 1, MOBA_BLOCK), F32),
                        pltpu.VMEM((N_HEADS, HEAD_DIM + L_ROWS, MOBA_BLOCK), F32),
                        pltpu.VMEM((ATT_WIDTH, MOBA_BLOCK), F32),
                        pltpu.VMEM((N_HEADS, MOBA_BLOCK, MOBA_BLOCK), F32),
                        pltpu.VMEM((N_HEADS, 1, MOBA_BLOCK), F32)],
        compiler_params=pltpu.CompilerParams(dimension_semantics=("arbitrary", "arbitrary"),
                                             vmem_limit_bytes=VMEM_LIMIT_BYTES),
        name="prompt_attn",
    )(qpad, kb.reshape(batch, nblk, MOBA_BLOCK, KV_WIDTH), vt.reshape(batch, nblk, KV_WIDTH, MOBA_BLOCK),
      kmean.reshape(batch, nblk, KV_WIDTH), sga, gmg, x, w_out16, g_post)


def _sample_attn_kernel(pt_ref, qbd_ref, knew_ref, vnew_ref, *refs, pages_per_step, n_past_blocks, page_size):
    k_pages = refs[:pages_per_step]
    v_pages = refs[pages_per_step:2 * pages_per_step]
    out_ref, o_ref, m_ref, l_ref, g_ref = refs[2 * pages_per_step:]
    s_idx = pl.program_id(1)
    n_steps = pl.num_programs(1)
    pages_per_block = MOBA_BLOCK // page_size
    blocks_per_step = pages_per_step // pages_per_block
    nq = qbd_ref.shape[0]
    t_new = knew_ref.shape[0]
    lane = lax.broadcasted_iota(jnp.int32, (nq, LANES), 1)

    @pl.when(s_idx == 0)
    def _():
        m_ref[...] = jnp.full((nq, LANES), NEG_INF, F32)
        l_ref[...] = jnp.zeros((nq, LANES), F32)
        g_ref[...] = jnp.full((nq, LANES), NEG_INF, F32)

    qbd = qbd_ref[...]
    qbd16 = qbd.astype(BF16)
    for jj in range(blocks_per_step):
        pages = range(jj * pages_per_block, (jj + 1) * pages_per_block)
        ktb = jnp.concatenate([k_pages[r][...] for r in pages], axis=1).astype(BF16)
        vtb = jnp.concatenate([v_pages[r][...] for r in pages], axis=1).astype(BF16)
        blk = s_idx * blocks_per_step + jj
        s = _dot(qbd16, ktb)
        gate = jnp.sum(s, axis=-1, keepdims=True) * np.float32(1.0 / MOBA_BLOCK)
        m = jnp.max(s, axis=-1, keepdims=True)
        p = jnp.exp2(s - m)
        o_ref[blk] = _dot_nt(p.astype(BF16), vtb)
        hit = lane == blk
        m_ref[...] = jnp.where(hit, m, m_ref[...])
        l_ref[...] = jnp.where(hit, jnp.sum(p, axis=-1, keepdims=True), l_ref[...])
        g_ref[...] = jnp.where(hit, gate, g_ref[...])

    @pl.when(s_idx == n_steps - 1)
    def _():
        g = g_ref[...]
        lane_f = lane.astype(F32)
        sel = jnp.zeros((nq, LANES), jnp.bool_)
        for _ in range(min(MOBA_TOP_K, n_past_blocks)):
            best = jnp.max(g, axis=-1, keepdims=True)
            idx = jnp.min(jnp.where(g == best, lane_f, np.float32(LANES)), axis=-1, keepdims=True)
            pick = lane_f == idx
            sel = sel | pick
            g = jnp.where(pick, NEG_INF, g)
        row = lax.broadcasted_iota(jnp.int32, (nq, LANES), 0)
        rows_per_token = nq // t_new
        s_own = jnp.full((nq, LANES), NEG_INF, F32)
        for t in range(t_new):
            s_t = jnp.sum(qbd * knew_ref[t:t + 1, :], axis=-1, keepdims=True)
            s_own = jnp.where((lane == t) & (row >= t * rows_per_token), s_t, s_own)
        m_blocks = m_ref[...]
        m_tot = jnp.maximum(jnp.max(jnp.where(sel, m_blocks, NEG_INF), axis=-1, keepdims=True),
                            jnp.max(s_own, axis=-1, keepdims=True))
        w = jnp.where(sel, jnp.exp2(m_blocks - m_tot), 0.0)
        p_own = jnp.exp2(s_own - m_tot)
        l_tot = (jnp.sum(w * l_ref[...], axis=-1, keepdims=True) + jnp.sum(p_own, axis=-1, keepdims=True))
        acc = jnp.zeros((nq, KV_WIDTH), F32)
        for n in range(n_past_blocks):
            acc = acc + jnp.sum(jnp.where(lane == n, w, 0.0), axis=-1, keepdims=True) * o_ref[n]
        for t in range(t_new):
            acc = acc + jnp.sum(jnp.where(lane == t, p_own, 0.0), axis=-1, keepdims=True) * vnew_ref[t:t + 1, :]
        out_ref[...] = acc / l_tot


def _sample_attn(page_table, qbd, k_new, v_new, cache_k, cache_v):
    db, n_pages = page_table.shape
    page_size = cache_k.shape[2]
    nq = qbd.shape[1]
    t_new = k_new.shape[1]
    pps = min(PAGES_PER_STEP, n_pages)
    pages_per_block = MOBA_BLOCK // page_size
    assert n_pages % pps == 0 and pps % pages_per_block == 0
    n_past_blocks = n_pages // pages_per_block
    assert n_past_blocks <= LANES and t_new <= LANES

    def page_spec(r):
        return pl.BlockSpec((None, KV_WIDTH, page_size), lambda b, s, pt: (pt[b, s * pps + r], 0, 0))

    per_seq = lambda rows: pl.BlockSpec((None, rows, KV_WIDTH), lambda b, s, pt: (b, 0, 0))
    kernel = functools.partial(_sample_attn_kernel, pages_per_step=pps, n_past_blocks=n_past_blocks,
                               page_size=page_size)
    return pl.pallas_call(
        kernel,
        out_shape=jax.ShapeDtypeStruct((db, nq, KV_WIDTH), F32),
        grid_spec=pltpu.PrefetchScalarGridSpec(
            num_scalar_prefetch=1,
            grid=(db, n_pages // pps),
            in_specs=([per_seq(nq), per_seq(t_new), per_seq(t_new)]
                      + [page_spec(r) for r in range(pps)] + [page_spec(r) for r in range(pps)]),
            out_specs=per_seq(nq),
            scratch_shapes=[pltpu.VMEM((n_past_blocks, nq, KV_WIDTH), F32),
                            pltpu.VMEM((nq, LANES), F32),
                            pltpu.VMEM((nq, LANES), F32),
                            pltpu.VMEM((nq, LANES), F32)]),
        compiler_params=pltpu.CompilerParams(dimension_semantics=("arbitrary", "arbitrary"),
                                             vmem_limit_bytes=VMEM_LIMIT_BYTES),
        name="sample_attn",
    )(page_table, qbd, k_new, v_new, *([cache_k] * pps), *([cache_v] * pps))


def _sample_tail_kernel(att_ref, sga_ref, gmg_ref, x_ref, wout_ref, gpost_ref, y_ref):
    y_ref[...] = _tail(att_ref[...], sga_ref, gmg_ref, x_ref, wout_ref, gpost_ref)


def _sample_tail(att, sga, gmg, x, w_out16, g_post):
    return pl.pallas_call(
        _sample_tail_kernel,
        out_shape=jax.ShapeDtypeStruct(x.shape, F32),
        compiler_params=pltpu.CompilerParams(vmem_limit_bytes=VMEM_LIMIT_BYTES),
        name="sample_tail",
    )(att, sga, gmg, x, w_out16, g_post)


def _pair_cat(w):
    g, c, _ = w.shape
    return w.reshape(g // 2, 2, c, c).transpose(0, 2, 1, 3).reshape(g // 2, c, 2 * c)


def _layer(x_p, x_s, cache_k, cache_v, page_table, g_pre, w_in, g_sgu, w_sp, b_sp, w_out, g_post):
    batch, seq, d_model = x_p.shape
    db, t_new, _ = x_s.shape
    assert seq % MOBA_BLOCK == 0 and GM_CHUNK % t_new == 0
    w_in16 = w_in.astype(BF16)
    w_out16 = w_out.astype(BF16)
    g_pre2, g_sgu2, g_post2 = g_pre[None, :], g_sgu[None, :], g_post[None, :]
    grp = jnp.arange(GM_WIDTH) // GM_GROUP_DIM
    gavg = jnp.where(grp[:, None] == grp[None, :], 1.0 / GM_GROUP_DIM, 0.0).astype(BF16)
    bias_p = jnp.repeat(b_sp.T, GM_GROUP_DIM, axis=1)
    wcat_p = _pair_cat(w_sp)
    seqs = GM_CHUNK // t_new
    w_s = jnp.einsum('ab,gts->gatbs', jnp.eye(seqs, dtype=F32), w_sp[:, :t_new, :t_new])
    wcat_s = _pair_cat(w_s.reshape(GM_GROUPS, GM_CHUNK, GM_CHUNK))
    bias_s = jnp.tile(bias_p[:t_new], (seqs, 1))

    xp2 = x_p.reshape(batch * seq, d_model)
    qpad, kt_p, vt_p, kb, vt, kmean, sga_p, gmg_p = _in_proj(
        xp2, g_pre2, w_in16, g_sgu2, gavg, wcat_p, bias_p, rows=PROMPT_ROWS, sample=False, seq=seq)
    y_p = _prompt_attn(qpad, kb, vt, kmean, sga_p, gmg_p, xp2, w_out16, g_post2, batch=batch, seq=seq)

    xs2 = x_s.reshape(db * t_new, d_model)
    q_s, k_s, v_s, sga_s, gmg_s, vn_s = _in_proj(
        xs2, g_pre2, w_in16, g_sgu2, gavg, wcat_s, bias_s, rows=db * t_new, sample=True)
    q5 = q_s.reshape(db, t_new, N_KV_HEADS, KV_GROUP, HEAD_DIM)
    eye = jnp.eye(N_KV_HEADS, dtype=F32)[None, None, :, None, :, None]
    qbd = (q5[:, :, :, :, None, :] * eye).reshape(db, t_new * N_HEADS, KV_WIDTH)
    n_phys, page_size = cache_k.shape[:2]
    ck = cache_k.transpose(0, 2, 3, 1).reshape(n_phys, KV_WIDTH, page_size)
    cv = cache_v.transpose(0, 2, 3, 1).reshape(n_phys, KV_WIDTH, page_size)
    o_full = _sample_attn(page_table, qbd, k_s.reshape(db, t_new, KV_WIDTH), v_s.reshape(db, t_new, KV_WIDTH), ck, cv)
    o6 = o_full.reshape(db, t_new, N_KV_HEADS, KV_GROUP, N_KV_HEADS, HEAD_DIM)
    att_s = jnp.stack([o6[:, :, kh, :, kh, :] for kh in range(N_KV_HEADS)], axis=2)
    att_s = att_s.reshape(db * t_new, ATT_WIDTH)
    y_s = _sample_tail(att_s, sga_s, gmg_s, xs2, w_out16, g_post2)

    rows_of = lambda t: t.reshape(batch, N_KV_HEADS, HEAD_DIM, seq).transpose(0, 3, 1, 2)
    kv_s = (db, t_new, N_KV_HEADS, HEAD_DIM)
    return (y_p.reshape(batch, seq, d_model), y_s.reshape(db, t_new, d_model),
            rows_of(kt_p), rows_of(vt_p), k_s.reshape(kv_s), v_s.reshape(kv_s),
            vn_s.reshape(db, t_new, GM_WIDTH))


def kernel(x_prompt, x_sample, cache_k, cache_v, page_table, g_pre, w_in, g_sgu, w_spatial, b_spatial, w_out, g_post):
    yp, ys = x_prompt, x_sample
    outs = []
    for l in range(w_in.shape[0]):
        yp, ys, kp, vp, kn, vn, sv = _layer(yp, ys, cache_k[l], cache_v[l], page_table, g_pre[l], w_in[l],
                                            g_sgu[l], w_spatial[l], b_spatial[l], w_out[l], g_post[l])
        outs.append((kp, vp, kn, vn, sv))
    stacked = [jnp.stack(parts) for parts in zip(*outs)]
    return (yp, ys, *stacked)
```

```python
import functools

import numpy as np
import jax
import jax.numpy as jnp
from jax import lax
from jax.experimental import pallas as pl
from jax.experimental.pallas import tpu as pltpu

N_HEADS = 8
HEAD_DIM = 64
N_KV_HEADS = 4
KV_GROUP = N_HEADS // N_KV_HEADS
ATT_WIDTH = N_HEADS * HEAD_DIM
KV_WIDTH = N_KV_HEADS * HEAD_DIM
MOBA_BLOCK = 256
MOBA_TOP_K = 3
GM_GROUPS = 8
GM_GROUP_DIM = 64
GM_WIDTH = GM_GROUPS * GM_GROUP_DIM
GM_CHUNK = 128
NORM_EPS = 1e-6
LN_EPS = 1e-5

LANES = 128
VMEM_LIMIT_BYTES = 56 * 1024 * 1024
PROMPT_ROWS = 512
PAGES_PER_STEP = 16
L_ROWS = 16

F32 = jnp.float32
BF16 = jnp.bfloat16
NEG_INF = float("-inf")
Q_SCALE = np.float32(HEAD_DIM ** -0.5 * np.log2(np.e))


def _dot(a, b):
    return jnp.dot(a, b, preferred_element_type=F32)


def _dot_nt(a, b):
    return lax.dot_general(a, b, (((1,), (1,)), ((), ())), preferred_element_type=F32)


def _gelu_tanh(x):
    c = np.float32(np.sqrt(2.0 / np.pi))
    return x * (0.5 * (1.0 + jnp.tanh(c * (x + np.float32(0.044715) * (x * x * x)))))


def _silu(x):
    return x / (1.0 + jnp.exp(-x))


def _rms_norm(x, g):
    ms = jnp.mean(x * x, axis=-1, keepdims=True)
    return x * lax.rsqrt(ms + NORM_EPS) * g


def _in_proj_kernel(x_ref, gpre_ref, win_ref, gsgu_ref, gavg_ref, wcat_ref, bias_ref, *out_refs, sample):
    if sample:
        q_ref, k_ref, v_ref, sga_ref, gmg_ref, vn_ref = out_refs
    else:
        q_ref, k_ref, v_ref, kb_ref, vt_ref, kmean_ref, sga_ref, gmg_ref = out_refs
    rows = x_ref.shape[0]
    h = _rms_norm(x_ref[...], gpre_ref[...]).astype(BF16)

    def proj(lo, width):
        return _dot(h, win_ref[:, lo:lo + width])

    o_k = ATT_WIDTH
    o_v = o_k + KV_WIDTH
    o_ga = o_v + KV_WIDTH
    o_u = o_ga + ATT_WIDTH
    o_vs = o_u + GM_WIDTH
    o_gg = o_vs + GM_WIDTH

    zq = proj(0, ATT_WIDTH) * Q_SCALE
    if sample:
        q_ref[...] = zq
    else:
        first_half = lax.broadcasted_iota(jnp.int32, (rows, LANES), 1) < HEAD_DIM
        for kh in range(N_KV_HEADS):
            tile = zq[:, kh * LANES:(kh + 1) * LANES]
            swapped = pltpu.roll(tile, HEAD_DIM, axis=1)
            keep = first_half if kh % 2 == 0 else jnp.logical_not(first_half)
            even_src, odd_src = (tile, swapped) if kh % 2 == 0 else (swapped, tile)
            h0 = 2 * kh
            q_ref[:, h0 * LANES:(h0 + 1) * LANES] = jnp.where(keep, even_src, 0.0).astype(BF16)
            q_ref[:, (h0 + 1) * LANES:(h0 + 2) * LANES] = jnp.where(keep, odd_src, 0.0).astype(BF16)

    zk = proj(o_k, KV_WIDTH)
    zv = proj(o_v, KV_WIDTH)
    if sample:
        k_ref[...] = zk
        v_ref[...] = zv
    else:
        k_ref[...] = zk.T
        zvt = zv.T
        v_ref[...] = zvt
        kb_ref[...] = zk.astype(BF16)
        for c in range(rows // MOBA_BLOCK):
            blk = slice(c * MOBA_BLOCK, (c + 1) * MOBA_BLOCK)
            kmean_ref[c] = jnp.sum(zk[blk], axis=0, keepdims=True) * np.float32(1.0 / MOBA_BLOCK)
            vt_ref[c] = zvt[:, blk].astype(BF16)

    sga_ref[...] = _silu(proj(o_ga, ATT_WIDTH)).astype(BF16)

    u = _gelu_tanh(proj(o_u, GM_WIDTH))
    gv = _gelu_tanh(proj(o_vs, GM_WIDTH))
    mu = _dot(gv.astype(BF16), gavg_ref[...])
    d = gv - mu
    var = _dot((d * d).astype(BF16), gavg_ref[...])
    vn = d * lax.rsqrt(var + LN_EPS) * gsgu_ref[...]
    if sample:
        vn_ref[...] = vn

    r_i = lax.broadcasted_iota(jnp.int32, (GM_CHUNK, 2 * GM_CHUNK), 0)
    c_i = lax.broadcasted_iota(jnp.int32, (GM_CHUNK, 2 * GM_CHUNK), 1)
    tril = r_i >= (c_i & (GM_CHUNK - 1))
    lane_lo = lax.broadcasted_iota(jnp.int32, (GM_CHUNK, LANES), 1) < GM_GROUP_DIM
    sgg = _silu(proj(o_gg, GM_WIDTH))
    for p in range(GM_GROUPS // 2):
        w_pair = jnp.where(tril, wcat_ref[p], 0.0).astype(BF16)
        cols = slice(p * LANES, (p + 1) * LANES)
        for c in range(rows // GM_CHUNK):
            rws = slice(c * GM_CHUNK, (c + 1) * GM_CHUNK)
            vp = vn[rws, cols]
            rhs = jnp.concatenate([jnp.where(lane_lo, vp, 0.0).astype(BF16),
                                   jnp.where(lane_lo, 0.0, vp).astype(BF16)], axis=0)
            mixed = _dot(w_pair, rhs) + bias_ref[:, cols]
            gmg_ref[rws, cols] = (u[rws, cols] * mixed * sgg[rws, cols]).astype(BF16)


def _in_proj(x, g_pre, w_in16, g_sgu, gavg, wcat, bias, *, rows, sample, seq=None):
    n, d_model = x.shape
    assert n % rows == 0 and rows % GM_CHUNK == 0
    steps = n // rows
    row_spec = lambda width: pl.BlockSpec((rows, width), lambda i: (i, 0))
    full = lambda a: pl.BlockSpec(a.shape, lambda i: (0,) * a.ndim)
    if sample:
        out_shape = [jax.ShapeDtypeStruct((n, ATT_WIDTH), F32),
                     jax.ShapeDtypeStruct((n, KV_WIDTH), F32),
                     jax.ShapeDtypeStruct((n, KV_WIDTH), F32),
                     jax.ShapeDtypeStruct((n, ATT_WIDTH), BF16),
                     jax.ShapeDtypeStruct((n, GM_WIDTH), BF16),
                     jax.ShapeDtypeStruct((n, GM_WIDTH), F32)]
        out_specs = [row_spec(ATT_WIDTH), row_spec(KV_WIDTH), row_spec(KV_WIDTH),
                     row_spec(ATT_WIDTH), row_spec(GM_WIDTH), row_spec(GM_WIDTH)]
    else:
        assert rows % MOBA_BLOCK == 0 and seq % rows == 0
        bps = rows // MOBA_BLOCK
        nblk = n // MOBA_BLOCK
        spb = seq // rows
        kv_t_spec = pl.BlockSpec((None, KV_WIDTH, rows), lambda i: (i // spb, 0, i % spb))
        out_shape = [jax.ShapeDtypeStruct((n, N_HEADS * LANES), BF16),
                     jax.ShapeDtypeStruct((n // seq, KV_WIDTH, seq), F32),
                     jax.ShapeDtypeStruct((n // seq, KV_WIDTH, seq), F32),
                     jax.ShapeDtypeStruct((n, KV_WIDTH), BF16),
                     jax.ShapeDtypeStruct((nblk, KV_WIDTH, MOBA_BLOCK), BF16),
                     jax.ShapeDtypeStruct((nblk, 1, KV_WIDTH), F32),
                     jax.ShapeDtypeStruct((n, ATT_WIDTH), BF16),
                     jax.ShapeDtypeStruct((n, GM_WIDTH), BF16)]
        out_specs = [row_spec(N_HEADS * LANES), kv_t_spec, kv_t_spec, row_spec(KV_WIDTH),
                     pl.BlockSpec((bps, KV_WIDTH, MOBA_BLOCK), lambda i: (i, 0, 0)),
                     pl.BlockSpec((bps, 1, KV_WIDTH), lambda i: (i, 0, 0)),
                     row_spec(ATT_WIDTH), row_spec(GM_WIDTH)]
    return pl.pallas_call(
        functools.partial(_in_proj_kernel, sample=sample),
        out_shape=out_shape,
        grid=(steps,),
        in_specs=[row_spec(d_model), full(g_pre), full(w_in16), full(g_sgu), full(gavg), full(wcat), full(bias)],
        out_specs=out_specs,
        compiler_params=pltpu.CompilerParams(dimension_semantics=("arbitrary",),
                                             vmem_limit_bytes=VMEM_LIMIT_BYTES),
        name="in_proj_sample" if sample else "in_proj_prompt",
    )(x, g_pre, w_in16, g_sgu, gavg, wcat, bias)


def _tail(att, sga_ref, gmg_ref, x_ref, wout_ref, gpost_ref):
    mix = jnp.concatenate([(att * sga_ref[...].astype(F32)).astype(BF16), gmg_ref[...]], axis=-1)
    out = _dot(mix, wout_ref[...])
    return x_ref[...] + _rms_norm(out, gpost_ref[...])


def _prompt_attn_kernel(q_ref, kb_ref, vt_ref, kmean_ref, sga_ref, gmg_ref, x_ref, wout_ref, gpost_ref,
                        y_ref, sel_ref, g_ref, m_ref, acc_ref, attt_ref, s_ref, bm_ref):
    i = pl.program_id(1)
    nblk = kb_ref.shape[0]
    tq = q_ref.shape[0]
    ones_rows = jnp.ones((L_ROWS, MOBA_BLOCK), BF16)

    def q_head(h):
        return q_ref[:, h * LANES:(h + 1) * LANES]

    def k_tile(j, h):
        a = (h // KV_GROUP) // 2
        return kb_ref[j, :, a * LANES:(a + 1) * LANES]

    def vt_aug(j, h):
        kh = h // KV_GROUP
        return jnp.concatenate([vt_ref[j, kh * HEAD_DIM:(kh + 1) * HEAD_DIM, :], ones_rows], axis=0)

    blk_i = lax.broadcasted_iota(jnp.int32, (nblk, tq), 0)
    km = kmean_ref[...]
    for h in range(N_HEADS):
        a = (h // KV_GROUP) // 2
        km_a = km[:, a * LANES:(a + 1) * LANES]
        km_hi = km_a.astype(BF16)
        km_lo = (km_a - km_hi.astype(F32)).astype(BF16)
        g2 = _dot_nt(jnp.concatenate([km_hi, km_lo], axis=0), q_head(h))
        g_ref[h] = g2[:nblk] + g2[nblk:]

    def count_beaten(n2, ranks):
        tie_wins = blk_i > n2
        out = []
        for h in range(N_HEADS):
            g = g_ref[h]
            gb = g_ref[h, pl.ds(n2, 1), :]
            out.append(ranks[h] + jnp.where(tie_wins, jnp.where(g > gb, 0.0, 1.0), jnp.where(gb > g, 1.0, 0.0)))
        return tuple(out)

    ranks = lax.fori_loop(0, i, count_beaten, tuple(jnp.zeros((nblk, tq), F32) for _ in range(N_HEADS)))
    for h in range(N_HEADS):
        sel_ref[h] = jnp.where((blk_i < i) & (ranks[h] < MOBA_TOP_K), 1.0, 0.0)

    causal = (lax.broadcasted_iota(jnp.int32, (MOBA_BLOCK, tq), 0)
              <= lax.broadcasted_iota(jnp.int32, (MOBA_BLOCK, tq), 1))

    def scores(j, mask):
        for h in range(N_HEADS):
            s = _dot_nt(k_tile(j, h), q_head(h))
            if mask is not None:
                s = jnp.where(mask, s, NEG_INF)
            s_ref[h] = s
            bm_ref[h] = jnp.max(s, axis=0, keepdims=True)

    scores(i, causal)
    for h in range(N_HEADS):
        m = bm_ref[h]
        p = jnp.exp2(s_ref[h] - m).astype(BF16)
        m_ref[h] = m
        acc_ref[h] = _dot(vt_aug(i, h), p)

    def past_block(j, carry):
        scores(j, None)
        for h in range(N_HEADS):
            chosen = sel_ref[h, pl.ds(j, 1), :] > 0.5
            m_old = m_ref[h]
            m_new = jnp.where(chosen, jnp.maximum(m_old, bm_ref[h]), m_old)
            p = jnp.exp2(s_ref[h] - jnp.where(chosen, m_new, jnp.inf)).astype(BF16)
            acc_ref[h] = acc_ref[h] * jnp.exp2(m_old - m_new) + _dot(vt_aug(j, h), p)
            m_ref[h] = m_new
        return carry

    lax.fori_loop(0, i, past_block, 0)

    for h in range(N_HEADS):
        acc = acc_ref[h]
        attt_ref[h * HEAD_DIM:(h + 1) * HEAD_DIM, :] = acc[:HEAD_DIM] / acc[HEAD_DIM:HEAD_DIM + 1]
    y_ref[...] = _tail(attt_ref[...].T, sga_ref, gmg_ref, x_ref, wout_ref, gpost_ref)


def _prompt_attn(qpad, kb, vt, kmean, sga, gmg, x, w_out16, g_post, *, batch, seq):
    nblk = seq // MOBA_BLOCK
    d_model = x.shape[-1]
    tile = lambda width: pl.BlockSpec((MOBA_BLOCK, width), lambda b, i: (b * nblk + i, 0))
    per_batch = lambda r, c: pl.BlockSpec((None, nblk, r, c), lambda b, i: (b, 0, 0, 0))
    full = lambda a: pl.BlockSpec(a.shape, lambda b, i: (0,) * a.ndim)
    return pl.pallas_call(
        _prompt_attn_kernel,
        out_shape=jax.ShapeDtypeStruct(x.shape, F32),
        grid=(batch, nblk),
        in_specs=[tile(N_HEADS * LANES),
                  per_batch(MOBA_BLOCK, KV_WIDTH),
                  per_batch(KV_WIDTH, MOBA_BLOCK),
                  pl.BlockSpec((None, nblk, KV_WIDTH), lambda b, i: (b, 0, 0)),
                  tile(ATT_WIDTH), tile(GM_WIDTH), tile(d_model), full(w_out16), full(g_post)],
        out_specs=tile(d_model),
        scratch_shapes=[pltpu.VMEM((N_HEADS, nblk, MOBA_BLOCK), F32),
                        pltpu.VMEM((N_HEADS, nblk, MOBA_BLOCK), F32),
                        pltpu.VMEM((N_HEADS, 1, MOBA_BLOCK), F32),
                        pltpu.VMEM((N_HEADS, HEAD_DIM + L_ROWS, MOBA_BLOCK), F32),
                        pltpu.VMEM((ATT_WIDTH, MOBA_BLOCK), F32),
                        pltpu.VMEM((N_HEADS, MOBA_BLOCK, MOBA_BLOCK), F32),
                        pltpu.VMEM((N_HEADS, 1, MOBA_BLOCK), F32)],
        compiler_params=pltpu.CompilerParams(dimension_semantics=("arbitrary", "arbitrary"),
                                             vmem_limit_bytes=VMEM_LIMIT_BYTES),
        name="prompt_attn",
    )(qpad, kb.reshape(batch, nblk, MOBA_BLOCK, KV_WIDTH), vt.reshape(batch, nblk, KV_WIDTH, MOBA_BLOCK),
      kmean.reshape(batch, nblk, KV_WIDTH), sga, gmg, x, w_out16, g_post)


def _sample_attn_kernel(pt_ref, qbd_ref, knew_ref, vnew_ref, *refs, pages_per_step, n_past_blocks, page_size):
    k_pages = refs[:pages_per_step]
    v_pages = refs[pages_per_step:2 * pages_per_step]
    out_ref, o_ref, m_ref, l_ref, g_ref, kt16_ref, vt16_ref = refs[2 * pages_per_step:]
    s_idx = pl.program_id(1)
    n_steps = pl.num_programs(1)
    pages_per_block = MOBA_BLOCK // page_size
    blocks_per_step = pages_per_step // pages_per_block
    nq = qbd_ref.shape[0]
    t_new = knew_ref.shape[0]
    lane = lax.broadcasted_iota(jnp.int32, (nq, LANES), 1)

    @pl.when(s_idx == 0)
    def _():
        m_ref[...] = jnp.full((nq, LANES), NEG_INF, F32)
        l_ref[...] = jnp.zeros((nq, LANES), F32)
        g_ref[...] = jnp.full((nq, LANES), NEG_INF, F32)

    qbd = qbd_ref[...]
    qbd16 = qbd.astype(BF16)
    for jj in range(blocks_per_step):
        for r in range(pages_per_block):
            cols = slice(r * page_size, (r + 1) * page_size)
            kt16_ref[jj, :, cols] = k_pages[jj * pages_per_block + r][...].astype(BF16)
            vt16_ref[jj, :, cols] = v_pages[jj * pages_per_block + r][...].astype(BF16)
    s_blk = [_dot(qbd16, kt16_ref[jj]) for jj in range(blocks_per_step)]
    m_blk = [jnp.max(s, axis=-1, keepdims=True) for s in s_blk]
    g_blk = [jnp.sum(s, axis=-1, keepdims=True) * np.float32(1.0 / MOBA_BLOCK) for s in s_blk]
    p_blk = [jnp.exp2(s - m) for s, m in zip(s_blk, m_blk)]
    l_blk = [jnp.sum(p, axis=-1, keepdims=True) for p in p_blk]
    m_all, l_all, g_all = m_ref[...], l_ref[...], g_ref[...]
    for jj in range(blocks_per_step):
        blk = s_idx * blocks_per_step + jj
        o_ref[blk] = _dot_nt(p_blk[jj].astype(BF16), vt16_ref[jj])
        hit = lane == blk
        m_all = jnp.where(hit, m_blk[jj], m_all)
        l_all = jnp.where(hit, l_blk[jj], l_all)
        g_all = jnp.where(hit, g_blk[jj], g_all)
    m_ref[...], l_ref[...], g_ref[...] = m_all, l_all, g_all

    @pl.when(s_idx == n_steps - 1)
    def _():
        g = g_ref[...]
        lane_f = lane.astype(F32)
        sel = jnp.zeros((nq, LANES), jnp.bool_)
        for _ in range(min(MOBA_TOP_K, n_past_blocks)):
            best = jnp.max(g, axis=-1, keepdims=True)
            idx = jnp.min(jnp.where(g == best, lane_f, np.float32(LANES)), axis=-1, keepdims=True)
            pick = lane_f == idx
            sel = sel | pick
            g = jnp.where(pick, NEG_INF, g)
        row = lax.broadcasted_iota(jnp.int32, (nq, LANES), 0)
        rows_per_token = nq // t_new
        s_own = jnp.full((nq, LANES), NEG_INF, F32)
        for t in range(t_new):
            s_t = jnp.sum(qbd * knew_ref[t:t + 1, :], axis=-1, keepdims=True)
            s_own = jnp.where((lane == t) & (row >= t * rows_per_token), s_t, s_own)
        m_blocks = m_ref[...]
        m_tot = jnp.maximum(jnp.max(jnp.where(sel, m_blocks, NEG_INF), axis=-1, keepdims=True),
                            jnp.max(s_own, axis=-1, keepdims=True))
        w = jnp.where(sel, jnp.exp2(m_blocks - m_tot), 0.0)
        p_own = jnp.exp2(s_own - m_tot)
        l_tot = (jnp.sum(w * l_ref[...], axis=-1, keepdims=True) + jnp.sum(p_own, axis=-1, keepdims=True))
        acc = jnp.zeros((nq, KV_WIDTH), F32)
        for n in range(n_past_blocks):
            acc = acc + jnp.sum(jnp.where(lane == n, w, 0.0), axis=-1, keepdims=True) * o_ref[n]
        for t in range(t_new):
            acc = acc + jnp.sum(jnp.where(lane == t, p_own, 0.0), axis=-1, keepdims=True) * vnew_ref[t:t + 1, :]
        out_ref[...] = acc / l_tot


def _sample_attn(page_table, qbd, k_new, v_new, cache_k, cache_v):
    db, n_pages = page_table.shape
    page_size = cache_k.shape[2]
    nq = qbd.shape[1]
    t_new = k_new.shape[1]
    pps = min(PAGES_PER_STEP, n_pages)
    pages_per_block = MOBA_BLOCK // page_size
    assert n_pages % pps == 0 and pps % pages_per_block == 0
    n_past_blocks = n_pages // pages_per_block
    assert n_past_blocks <= LANES and t_new <= LANES

    def page_spec(r):
        return pl.BlockSpec((None, KV_WIDTH, page_size), lambda b, s, pt: (pt[b, s * pps + r], 0, 0))

    per_seq = lambda rows: pl.BlockSpec((None, rows, KV_WIDTH), lambda b, s, pt: (b, 0, 0))
    kernel = functools.partial(_sample_attn_kernel, pages_per_step=pps, n_past_blocks=n_past_blocks,
                               page_size=page_size)
    return pl.pallas_call(
        kernel,
        out_shape=jax.ShapeDtypeStruct((db, nq, KV_WIDTH), F32),
        grid_spec=pltpu.PrefetchScalarGridSpec(
            num_scalar_prefetch=1,
            grid=(db, n_pages // pps),
            in_specs=([per_seq(nq), per_seq(t_new), per_seq(t_new)]
                      + [page_spec(r) for r in range(pps)] + [page_spec(r) for r in range(pps)]),
            out_specs=per_seq(nq),
            scratch_shapes=[pltpu.VMEM((n_past_blocks, nq, KV_WIDTH), F32),
                            pltpu.VMEM((nq, LANES), F32),
                            pltpu.VMEM((nq, LANES), F32),
                            pltpu.VMEM((nq, LANES), F32),
                            pltpu.VMEM((pps // pages_per_block, KV_WIDTH, MOBA_BLOCK), BF16),
                            pltpu.VMEM((pps // pages_per_block, KV_WIDTH, MOBA_BLOCK), BF16)]),
        compiler_params=pltpu.CompilerParams(dimension_semantics=("arbitrary", "arbitrary"),
                                             vmem_limit_bytes=VMEM_LIMIT_BYTES),
        name="sample_attn",
    )(page_table, qbd, k_new, v_new, *([cache_k] * pps), *([cache_v] * pps))


def _sample_tail_kernel(att_ref, sga_ref, gmg_ref, x_ref, wout_ref, gpost_ref, y_ref):
    y_ref[...] = _tail(att_ref[...], sga_ref, gmg_ref, x_ref, wout_ref, gpost_ref)


def _sample_tail(att, sga, gmg, x, w_out16, g_post):
    return pl.pallas_call(
        _sample_tail_kernel,
        out_shape=jax.ShapeDtypeStruct(x.shape, F32),
        compiler_params=pltpu.CompilerParams(vmem_limit_bytes=VMEM_LIMIT_BYTES),
        name="sample_tail",
    )(att, sga, gmg, x, w_out16, g_post)


def _pair_cat(w):
    g, c, _ = w.shape
    return w.reshape(g // 2, 2, c, c).transpose(0, 2, 1, 3).reshape(g // 2, c, 2 * c)


def _layer(x_p, x_s, cache_k, cache_v, page_table, g_pre, w_in, g_sgu, w_sp, b_sp, w_out, g_post):
    batch, seq, d_model = x_p.shape
    db, t_new, _ = x_s.shape
    assert seq % MOBA_BLOCK == 0 and GM_CHUNK % t_new == 0
    w_in16 = w_in.astype(BF16)
    w_out16 = w_out.astype(BF16)
    g_pre2, g_sgu2, g_post2 = g_pre[None, :], g_sgu[None, :], g_post[None, :]
    grp = jnp.arange(GM_WIDTH) // GM_GROUP_DIM
    gavg = jnp.where(grp[:, None] == grp[None, :], 1.0 / GM_GROUP_DIM, 0.0).astype(BF16)
    bias_p = jnp.repeat(b_sp.T, GM_GROUP_DIM, axis=1)
    wcat_p = _pair_cat(w_sp)
    seqs = GM_CHUNK // t_new
    w_s = jnp.einsum('ab,gts->gatbs', jnp.eye(seqs, dtype=F32), w_sp[:, :t_new, :t_new])
    wcat_s = _pair_cat(w_s.reshape(GM_GROUPS, GM_CHUNK, GM_CHUNK))
    bias_s = jnp.tile(bias_p[:t_new], (seqs, 1))

    xp2 = x_p.reshape(batch * seq, d_model)
    qpad, kt_p, vt_p, kb, vt, kmean, sga_p, gmg_p = _in_proj(
        xp2, g_pre2, w_in16, g_sgu2, gavg, wcat_p, bias_p, rows=PROMPT_ROWS, sample=False, seq=seq)
    y_p = _prompt_attn(qpad, kb, vt, kmean, sga_p, gmg_p, xp2, w_out16, g_post2, batch=batch, seq=seq)

    xs2 = x_s.reshape(db * t_new, d_model)
    q_s, k_s, v_s, sga_s, gmg_s, vn_s = _in_proj(
        xs2, g_pre2, w_in16, g_sgu2, gavg, wcat_s, bias_s, rows=db * t_new, sample=True)
    q5 = q_s.reshape(db, t_new, N_KV_HEADS, KV_GROUP, HEAD_DIM)
    eye = jnp.eye(N_KV_HEADS, dtype=F32)[None, None, :, None, :, None]
    qbd = (q5[:, :, :, :, None, :] * eye).reshape(db, t_new * N_HEADS, KV_WIDTH)
    n_phys, page_size = cache_k.shape[:2]
    ck = cache_k.transpose(0, 2, 3, 1).reshape(n_phys, KV_WIDTH, page_size)
    cv = cache_v.transpose(0, 2, 3, 1).reshape(n_phys, KV_WIDTH, page_size)
    o_full = _sample_attn(page_table, qbd, k_s.reshape(db, t_new, KV_WIDTH), v_s.reshape(db, t_new, KV_WIDTH), ck, cv)
    o6 = o_full.reshape(db, t_new, N_KV_HEADS, KV_GROUP, N_KV_HEADS, HEAD_DIM)
    att_s = jnp.stack([o6[:, :, kh, :, kh, :] for kh in range(N_KV_HEADS)], axis=2)
    att_s = att_s.reshape(db * t_new, ATT_WIDTH)
    y_s = _sample_tail(att_s, sga_s, gmg_s, xs2, w_out16, g_post2)

    rows_of = lambda t: t.reshape(batch, N_KV_HEADS, HEAD_DIM, seq).transpose(0, 3, 1, 2)
    kv_s = (db, t_new, N_KV_HEADS, HEAD_DIM)
    return (y_p.reshape(batch, seq, d_model), y_s.reshape(db, t_new, d_model),
            rows_of(kt_p), rows_of(vt_p), k_s.reshape(kv_s), v_s.reshape(kv_s),
            vn_s.reshape(db, t_new, GM_WIDTH))


def kernel(x_prompt, x_sample, cache_k, cache_v, page_table, g_pre, w_in, g_sgu, w_spatial, b_spatial, w_out, g_post):
    yp, ys = x_prompt, x_sample
    outs = []
    for l in range(w_in.shape[0]):
        yp, ys, kp, vp, kn, vn, sv = _layer(yp, ys, cache_k[l], cache_v[l], page_table, g_pre[l], w_in[l],
                                            g_sgu[l], w_spatial[l], b_spatial[l], w_out[l], g_post[l])
        outs.append((kp, vp, kn, vn, sv))
    stacked = [jnp.stack(parts) for parts in zip(*outs)]
    return (yp, ys, *stacked)
```

```python
import functools

import numpy as np
import jax
import jax.numpy as jnp
from jax import lax
from jax.experimental import pallas as pl
from jax.experimental.pallas import tpu as pltpu

N_HEADS = 8
HEAD_DIM = 64
N_KV_HEADS = 4
KV_GROUP = N_HEADS // N_KV_HEADS
ATT_WIDTH = N_HEADS * HEAD_DIM
KV_WIDTH = N_KV_HEADS * HEAD_DIM
MOBA_BLOCK = 256
MOBA_TOP_K = 3
GM_GROUPS = 8
GM_GROUP_DIM = 64
GM_WIDTH = GM_GROUPS * GM_GROUP_DIM
GM_CHUNK = 128
NORM_EPS = 1e-6
LN_EPS = 1e-5

LANES = 128
VMEM_LIMIT_BYTES = 56 * 1024 * 1024
PROMPT_ROWS = 512
PAGES_PER_STEP = 16
L_ROWS = 16

F32 = jnp.float32
BF16 = jnp.bfloat16
NEG_INF = float("-inf")
Q_SCALE = np.float32(HEAD_DIM ** -0.5 * np.log2(np.e))


def _dot(a, b):
    return jnp.dot(a, b, preferred_element_type=F32)


def _dot_nt(a, b):
    return lax.dot_general(a, b, (((1,), (1,)), ((), ())), preferred_element_type=F32)


def _gelu_tanh(x):
    c = np.float32(np.sqrt(2.0 / np.pi))
    return x * (0.5 * (1.0 + jnp.tanh(c * (x + np.float32(0.044715) * (x * x * x)))))


def _silu(x):
    return x / (1.0 + jnp.exp(-x))


def _rms_norm(x, g):
    ms = jnp.mean(x * x, axis=-1, keepdims=True)
    return x * lax.rsqrt(ms + NORM_EPS) * g


def _in_proj_kernel(x_ref, gpre_ref, win_ref, gsgu_ref, gavg_ref, wcat_ref, bias_ref, *out_refs, sample):
    if sample:
        q_ref, k_ref, v_ref, sga_ref, gmg_ref, vn_ref = out_refs
    else:
        q_ref, k_ref, v_ref, kb_ref, vt_ref, kmean_ref, sga_ref, gmg_ref = out_refs
    rows = x_ref.shape[0]
    h = _rms_norm(x_ref[...], gpre_ref[...]).astype(BF16)

    def proj(lo, width):
        return _dot(h, win_ref[:, lo:lo + width])

    o_k = ATT_WIDTH
    o_v = o_k + KV_WIDTH
    o_ga = o_v + KV_WIDTH
    o_u = o_ga + ATT_WIDTH
    o_vs = o_u + GM_WIDTH
    o_gg = o_vs + GM_WIDTH

    zq = proj(0, ATT_WIDTH) * Q_SCALE
    if sample:
        q_ref[...] = zq
    else:
        first_half = lax.broadcasted_iota(jnp.int32, (rows, LANES), 1) < HEAD_DIM
        for kh in range(N_KV_HEADS):
            tile = zq[:, kh * LANES:(kh + 1) * LANES]
            swapped = pltpu.roll(tile, HEAD_DIM, axis=1)
            keep = first_half if kh % 2 == 0 else jnp.logical_not(first_half)
            even_src, odd_src = (tile, swapped) if kh % 2 == 0 else (swapped, tile)
            h0 = 2 * kh
            q_ref[:, h0 * LANES:(h0 + 1) * LANES] = jnp.where(keep, even_src, 0.0).astype(BF16)
            q_ref[:, (h0 + 1) * LANES:(h0 + 2) * LANES] = jnp.where(keep, odd_src, 0.0).astype(BF16)

    zk = proj(o_k, KV_WIDTH)
    zv = proj(o_v, KV_WIDTH)
    if sample:
        k_ref[...] = zk
        v_ref[...] = zv
    else:
        k_ref[...] = zk.T
        zvt = zv.T
        v_ref[...] = zvt
        kb_ref[...] = zk.astype(BF16)
        for c in range(rows // MOBA_BLOCK):
            blk = slice(c * MOBA_BLOCK, (c + 1) * MOBA_BLOCK)
            kmean_ref[c] = jnp.sum(zk[blk], axis=0, keepdims=True) * np.float32(1.0 / MOBA_BLOCK)
            vt_ref[c] = zvt[:, blk].astype(BF16)

    sga_ref[...] = _silu(proj(o_ga, ATT_WIDTH)).astype(BF16)

    u = _gelu_tanh(proj(o_u, GM_WIDTH))
    gv = _gelu_tanh(proj(o_vs, GM_WIDTH))
    mu = _dot(gv.astype(BF16), gavg_ref[...])
    d = gv - mu
    var = _dot((d * d).astype(BF16), gavg_ref[...])
    vn = d * lax.rsqrt(var + LN_EPS) * gsgu_ref[...]
    if sample:
        vn_ref[...] = vn

    r_i = lax.broadcasted_iota(jnp.int32, (GM_CHUNK, 2 * GM_CHUNK), 0)
    c_i = lax.broadcasted_iota(jnp.int32, (GM_CHUNK, 2 * GM_CHUNK), 1)
    tril = r_i >= (c_i & (GM_CHUNK - 1))
    lane_lo = lax.broadcasted_iota(jnp.int32, (GM_CHUNK, LANES), 1) < GM_GROUP_DIM
    sgg = _silu(proj(o_gg, GM_WIDTH))
    n_chunks = rows // GM_CHUNK
    chunks_per_dot = 2 if n_chunks % 2 == 0 else 1
    for p in range(GM_GROUPS // 2):
        w_pair = jnp.where(tril, wcat_ref[p], 0.0).astype(BF16)
        cols = slice(p * LANES, (p + 1) * LANES)

        def split_groups(c):
            vp = vn[c * GM_CHUNK:(c + 1) * GM_CHUNK, cols]
            return jnp.concatenate([jnp.where(lane_lo, vp, 0.0).astype(BF16),
                                    jnp.where(lane_lo, 0.0, vp).astype(BF16)], axis=0)

        for c0 in range(0, n_chunks, chunks_per_dot):
            cs = range(c0, c0 + chunks_per_dot)
            mixed_all = _dot(w_pair, jnp.concatenate([split_groups(c) for c in cs], axis=1))
            for k, c in enumerate(cs):
                rws = slice(c * GM_CHUNK, (c + 1) * GM_CHUNK)
                mixed = mixed_all[:, k * LANES:(k + 1) * LANES] + bias_ref[:, cols]
                gmg_ref[rws, cols] = (u[rws, cols] * mixed * sgg[rws, cols]).astype(BF16)


def _in_proj(x, g_pre, w_in16, g_sgu, gavg, wcat, bias, *, rows, sample, seq=None):
    n, d_model = x.shape
    assert n % rows == 0 and rows % GM_CHUNK == 0
    steps = n // rows
    row_spec = lambda width: pl.BlockSpec((rows, width), lambda i: (i, 0))
    full = lambda a: pl.BlockSpec(a.shape, lambda i: (0,) * a.ndim)
    if sample:
        out_shape = [jax.ShapeDtypeStruct((n, ATT_WIDTH), F32),
                     jax.ShapeDtypeStruct((n, KV_WIDTH), F32),
                     jax.ShapeDtypeStruct((n, KV_WIDTH), F32),
                     jax.ShapeDtypeStruct((n, ATT_WIDTH), BF16),
                     jax.ShapeDtypeStruct((n, GM_WIDTH), BF16),
                     jax.ShapeDtypeStruct((n, GM_WIDTH), F32)]
        out_specs = [row_spec(ATT_WIDTH), row_spec(KV_WIDTH), row_spec(KV_WIDTH),
                     row_spec(ATT_WIDTH), row_spec(GM_WIDTH), row_spec(GM_WIDTH)]
    else:
        assert rows % MOBA_BLOCK == 0 and seq % rows == 0
        bps = rows // MOBA_BLOCK
        nblk = n // MOBA_BLOCK
        spb = seq // rows
        kv_t_spec = pl.BlockSpec((None, KV_WIDTH, rows), lambda i: (i // spb, 0, i % spb))
        out_shape = [jax.ShapeDtypeStruct((n, N_HEADS * LANES), BF16),
                     jax.ShapeDtypeStruct((n // seq, KV_WIDTH, seq), F32),
                     jax.ShapeDtypeStruct((n // seq, KV_WIDTH, seq), F32),
                     jax.ShapeDtypeStruct((n, KV_WIDTH), BF16),
                     jax.ShapeDtypeStruct((nblk, KV_WIDTH, MOBA_BLOCK), BF16),
                     jax.ShapeDtypeStruct((nblk, 1, KV_WIDTH), F32),
                     jax.ShapeDtypeStruct((n, ATT_WIDTH), BF16),
                     jax.ShapeDtypeStruct((n, GM_WIDTH), BF16)]
        out_specs = [row_spec(N_HEADS * LANES), kv_t_spec, kv_t_spec, row_spec(KV_WIDTH),
                     pl.BlockSpec((bps, KV_WIDTH, MOBA_BLOCK), lambda i: (i, 0, 0)),
                     pl.BlockSpec((bps, 1, KV_WIDTH), lambda i: (i, 0, 0)),
                     row_spec(ATT_WIDTH), row_spec(GM_WIDTH)]
    return pl.pallas_call(
        functools.partial(_in_proj_kernel, sample=sample),
        out_shape=out_shape,
        grid=(steps,),
        in_specs=[row_spec(d_model), full(g_pre), full(w_in16), full(g_sgu), full(gavg), full(wcat), full(bias)],
        out_specs=out_specs,
        compiler_params=pltpu.CompilerParams(dimension_semantics=("arbitrary",),
                                             vmem_limit_bytes=VMEM_LIMIT_BYTES),
        name="in_proj_sample" if sample else "in_proj_prompt",
    )(x, g_pre, w_in16, g_sgu, gavg, wcat, bias)


def _tail(att, sga_ref, gmg_ref, x_ref, wout_ref, gpost_ref):
    mix = jnp.concatenate([(att * sga_ref[...].astype(F32)).astype(BF16), gmg_ref[...]], axis=-1)
    out = _dot(mix, wout_ref[...])
    return x_ref[...] + _rms_norm(out, gpost_ref[...])


def _prompt_attn_kernel(q_ref, kb_ref, vt_ref, kmean_ref, sga_ref, gmg_ref, x_ref, wout_ref, gpost_ref,
                        y_ref, sel_ref, g_ref, m_ref, acc_ref, attt_ref, sa_ref, sb_ref, bma_ref, bmb_ref):
    i = pl.program_id(1)
    nblk = kb_ref.shape[0]
    tq = q_ref.shape[0]
    ones_rows = jnp.ones((L_ROWS, MOBA_BLOCK), BF16)

    def q_head(h):
        return q_ref[:, h * LANES:(h + 1) * LANES]

    def k_tile(j, h):
        a = (h // KV_GROUP) // 2
        return kb_ref[j, :, a * LANES:(a + 1) * LANES]

    def vt_aug(j, h):
        kh = h // KV_GROUP
        return jnp.concatenate([vt_ref[j, kh * HEAD_DIM:(kh + 1) * HEAD_DIM, :], ones_rows], axis=0)

    blk_i = lax.broadcasted_iota(jnp.int32, (nblk, tq), 0)
    km = kmean_ref[...]
    for h in range(N_HEADS):
        a = (h // KV_GROUP) // 2
        km_a = km[:, a * LANES:(a + 1) * LANES]
        km_hi = km_a.astype(BF16)
        km_lo = (km_a - km_hi.astype(F32)).astype(BF16)
        g2 = _dot_nt(jnp.concatenate([km_hi, km_lo], axis=0), q_head(h))
        g_ref[h] = g2[:nblk] + g2[nblk:]

    def count_beaten(n2, ranks):
        tie_wins = blk_i > n2
        out = []
        for h in range(N_HEADS):
            g = g_ref[h]
            gb = g_ref[h, pl.ds(n2, 1), :]
            out.append(ranks[h] + jnp.where(tie_wins, jnp.where(g > gb, 0.0, 1.0), jnp.where(gb > g, 1.0, 0.0)))
        return tuple(out)

    ranks = lax.fori_loop(0, i, count_beaten, tuple(jnp.zeros((nblk, tq), F32) for _ in range(N_HEADS)))
    for h in range(N_HEADS):
        sel_ref[h] = jnp.where((blk_i < i) & (ranks[h] < MOBA_TOP_K), 1.0, 0.0)

    causal = (lax.broadcasted_iota(jnp.int32, (MOBA_BLOCK, tq), 0)
              <= lax.broadcasted_iota(jnp.int32, (MOBA_BLOCK, tq), 1))

    head_pairs = [(KV_GROUP * kh, KV_GROUP * kh + 1) for kh in range(N_KV_HEADS)]
    bufs = ((sa_ref, bma_ref), (sb_ref, bmb_ref))

    def scores(j, mask, buf, heads):
        s_buf, bm_buf = buf
        for h in heads:
            s = _dot_nt(k_tile(j, h), q_head(h))
            if mask is not None:
                s = jnp.where(mask, s, NEG_INF)
            s16 = s.astype(BF16)
            s_buf[h] = s16
            bm_buf[h] = jnp.max(s16, axis=0, keepdims=True).astype(F32)

    def attend_own(buf, heads):
        s_buf, bm_buf = buf
        for h in heads:
            m = bm_buf[h]
            m_ref[h] = m
            acc_ref[h] = _dot(vt_aug(i, h), jnp.exp2(s_buf[h] - m.astype(BF16)))

    def attend(j, buf, heads):
        s_buf, bm_buf = buf
        for h in heads:
            chosen = sel_ref[h, pl.ds(j, 1), :] > 0.5
            m_old = m_ref[h]
            m_new = jnp.where(chosen, jnp.maximum(m_old, bm_buf[h]), m_old)
            p = jnp.exp2(s_buf[h] - jnp.where(chosen, m_new, jnp.inf).astype(BF16))
            acc_ref[h] = acc_ref[h] * jnp.exp2(m_old - m_new) + _dot(vt_aug(j, h), p)
            m_ref[h] = m_new

    def overlapped(attend_pair, j_next, buf_next):
        scores(j_next, None, buf_next, head_pairs[0])
        for k, pair in enumerate(head_pairs):
            attend_pair(pair)
            if k + 1 < len(head_pairs):
                scores(j_next, None, buf_next, head_pairs[k + 1])

    scores(i, causal, bufs[0], range(N_HEADS))
    overlapped(lambda pair: attend_own(bufs[0], pair), 0, bufs[1])

    def past_pair(t, carry):
        j0 = 2 * t
        overlapped(lambda pair: attend(j0, bufs[1], pair), j0 + 1, bufs[0])
        overlapped(lambda pair: attend(j0 + 1, bufs[0], pair), j0 + 2, bufs[1])
        return carry

    lax.fori_loop(0, lax.shift_right_logical(i, jnp.int32(1)), past_pair, 0)

    @pl.when((i & 1) == 1)
    def _():
        attend(i - 1, bufs[1], range(N_HEADS))

    for h in range(N_HEADS):
        acc = acc_ref[h]
        attt_ref[h * HEAD_DIM:(h + 1) * HEAD_DIM, :] = acc[:HEAD_DIM] / acc[HEAD_DIM:HEAD_DIM + 1]
    y_ref[...] = _tail(attt_ref[...].T, sga_ref, gmg_ref, x_ref, wout_ref, gpost_ref)


def _prompt_attn(qpad, kb, vt, kmean, sga, gmg, x, w_out16, g_post, *, batch, seq):
    nblk = seq // MOBA_BLOCK
    d_model = x.shape[-1]
    tile = lambda width: pl.BlockSpec((MOBA_BLOCK, width), lambda b, i: (b * nblk + i, 0))
    per_batch = lambda r, c: pl.BlockSpec((None, nblk, r, c), lambda b, i: (b, 0, 0, 0))
    full = lambda a: pl.BlockSpec(a.shape, lambda b, i: (0,) * a.ndim)
    return pl.pallas_call(
        _prompt_attn_kernel,
        out_shape=jax.ShapeDtypeStruct(x.shape, F32),
        grid=(batch, nblk),
        in_specs=[tile(N_HEADS * LANES),
                  per_batch(MOBA_BLOCK, KV_WIDTH),
                  per_batch(KV_WIDTH, MOBA_BLOCK),
                  pl.BlockSpec((None, nblk, KV_WIDTH), lambda b, i: (b, 0, 0)),
                  tile(ATT_WIDTH), tile(GM_WIDTH), tile(d_model), full(w_out16), full(g_post)],
        out_specs=tile(d_model),
        scratch_shapes=[pltpu.VMEM((N_HEADS, nblk, MOBA_BLOCK), F32),
                        pltpu.VMEM((N_HEADS, nblk, MOBA_BLOCK), F32),
                        pltpu.VMEM((N_HEADS, 1, MOBA_BLOCK), F32),
                        pltpu.VMEM((N_HEADS, HEAD_DIM + L_ROWS, MOBA_BLOCK), F32),
                        pltpu.VMEM((ATT_WIDTH, MOBA_BLOCK), F32),
                        pltpu.VMEM((N_HEADS, MOBA_BLOCK, MOBA_BLOCK), BF16),
                        pltpu.VMEM((N_HEADS, MOBA_BLOCK, MOBA_BLOCK), BF16),
                        pltpu.VMEM((N_HEADS, 1, MOBA_BLOCK), F32),
                        pltpu.VMEM((N_HEADS, 1, MOBA_BLOCK), F32)],
        compiler_params=pltpu.CompilerParams(dimension_semantics=("arbitrary", "arbitrary"),
                                             vmem_limit_bytes=VMEM_LIMIT_BYTES),
        name="prompt_attn",
    )(qpad, kb.reshape(batch, nblk, MOBA_BLOCK, KV_WIDTH), vt.reshape(batch, nblk, KV_WIDTH, MOBA_BLOCK),
      kmean.reshape(batch, nblk, KV_WIDTH), sga, gmg, x, w_out16, g_post)


def _sample_attn_kernel(pt_ref, qbd_ref, knew_ref, vnew_ref, *refs, pages_per_step, n_past_blocks, page_size):
    k_pages = refs[:pages_per_step]
    v_pages = refs[pages_per_step:2 * pages_per_step]
    out_ref, o_ref, m_ref, l_ref, g_ref, kt16_ref, vt16_ref = refs[2 * pages_per_step:]
    s_idx = pl.program_id(1)
    n_steps = pl.num_programs(1)
    pages_per_block = MOBA_BLOCK // page_size
    blocks_per_step = pages_per_step // pages_per_block
    nq = qbd_ref.shape[0]
    t_new = knew_ref.shape[0]
    lane = lax.broadcasted_iota(jnp.int32, (nq, LANES), 1)

    @pl.when(s_idx == 0)
    def _():
        m_ref[...] = jnp.full((nq, LANES), NEG_INF, F32)
        l_ref[...] = jnp.zeros((nq, LANES), F32)
        g_ref[...] = jnp.full((nq, LANES), NEG_INF, F32)

    qbd = qbd_ref[...]
    qbd16 = qbd.astype(BF16)
    for jj in range(blocks_per_step):
        for r in range(pages_per_block):
            cols = slice(r * page_size, (r + 1) * page_size)
            kt16_ref[jj, :, cols] = k_pages[jj * pages_per_block + r][...].astype(BF16)
            vt16_ref[jj, :, cols] = v_pages[jj * pages_per_block + r][...].astype(BF16)
    s_blk = [_dot(qbd16, kt16_ref[jj]) for jj in range(blocks_per_step)]
    m_blk = [jnp.max(s, axis=-1, keepdims=True) for s in s_blk]
    g_blk = [jnp.sum(s, axis=-1, keepdims=True) * np.float32(1.0 / MOBA_BLOCK) for s in s_blk]
    p_blk = [jnp.exp2(s - m) for s, m in zip(s_blk, m_blk)]
    l_blk = [jnp.sum(p, axis=-1, keepdims=True) for p in p_blk]
    m_all, l_all, g_all = m_ref[...], l_ref[...], g_ref[...]
    for jj in range(blocks_per_step):
        blk = s_idx * blocks_per_step + jj
        o_ref[blk] = _dot_nt(p_blk[jj].astype(BF16), vt16_ref[jj])
        hit = lane == blk
        m_all = jnp.where(hit, m_blk[jj], m_all)
        l_all = jnp.where(hit, l_blk[jj], l_all)
        g_all = jnp.where(hit, g_blk[jj], g_all)
    m_ref[...], l_ref[...], g_ref[...] = m_all, l_all, g_all

    @pl.when(s_idx == n_steps - 1)
    def _():
        g = g_ref[...]
        lane_f = lane.astype(F32)
        sel = jnp.zeros((nq, LANES), jnp.bool_)
        for _ in range(min(MOBA_TOP_K, n_past_blocks)):
            best = jnp.max(g, axis=-1, keepdims=True)
            idx = jnp.min(jnp.where(g == best, lane_f, np.float32(LANES)), axis=-1, keepdims=True)
            pick = lane_f == idx
            sel = sel | pick
            g = jnp.where(pick, NEG_INF, g)
        row = lax.broadcasted_iota(jnp.int32, (nq, LANES), 0)
        rows_per_token = nq // t_new
        s_own = jnp.full((nq, LANES), NEG_INF, F32)
        for t in range(t_new):
            s_t = jnp.sum(qbd * knew_ref[t:t + 1, :], axis=-1, keepdims=True)
            s_own = jnp.where((lane == t) & (row >= t * rows_per_token), s_t, s_own)
        m_blocks = m_ref[...]
        m_tot = jnp.maximum(jnp.max(jnp.where(sel, m_blocks, NEG_INF), axis=-1, keepdims=True),
                            jnp.max(s_own, axis=-1, keepdims=True))
        w = jnp.where(sel, jnp.exp2(m_blocks - m_tot), 0.0)
        p_own = jnp.exp2(s_own - m_tot)
        l_tot = (jnp.sum(w * l_ref[...], axis=-1, keepdims=True) + jnp.sum(p_own, axis=-1, keepdims=True))
        acc = jnp.zeros((nq, KV_WIDTH), F32)
        for n in range(n_past_blocks):
            acc = acc + jnp.sum(jnp.where(lane == n, w, 0.0), axis=-1, keepdims=True) * o_ref[n]
        for t in range(t_new):
            acc = acc + jnp.sum(jnp.where(lane == t, p_own, 0.0), axis=-1, keepdims=True) * vnew_ref[t:t + 1, :]
        out_ref[...] = acc / l_tot


def _sample_attn(page_table, qbd, k_new, v_new, cache_k, cache_v):
    db, n_pages = page_table.shape
    page_size = cache_k.shape[2]
    nq = qbd.shape[1]
    t_new = k_new.shape[1]
    pps = min(PAGES_PER_STEP, n_pages)
    pages_per_block = MOBA_BLOCK // page_size
    assert n_pages % pps == 0 and pps % pages_per_block == 0
    n_past_blocks = n_pages // pages_per_block
    assert n_past_blocks <= LANES and t_new <= LANES

    def page_spec(r):
        return pl.BlockSpec((None, KV_WIDTH, page_size), lambda b, s, pt: (pt[b, s * pps + r], 0, 0))

    per_seq = lambda rows: pl.BlockSpec((None, rows, KV_WIDTH), lambda b, s, pt: (b, 0, 0))
    kernel = functools.partial(_sample_attn_kernel, pages_per_step=pps, n_past_blocks=n_past_blocks,
                               page_size=page_size)
    return pl.pallas_call(
        kernel,
        out_shape=jax.ShapeDtypeStruct((db, nq, KV_WIDTH), F32),
        grid_spec=pltpu.PrefetchScalarGridSpec(
            num_scalar_prefetch=1,
            grid=(db, n_pages // pps),
            in_specs=([per_seq(nq), per_seq(t_new), per_seq(t_new)]
                      + [page_spec(r) for r in range(pps)] + [page_spec(r) for r in range(pps)]),
            out_specs=per_seq(nq),
            scratch_shapes=[pltpu.VMEM((n_past_blocks, nq, KV_WIDTH), F32),
                            pltpu.VMEM((nq, LANES), F32),
                            pltpu.VMEM((nq, LANES), F32),
                            pltpu.VMEM((nq, LANES), F32),
                            pltpu.VMEM((pps // pages_per_block, KV_WIDTH, MOBA_BLOCK), BF16),
                            pltpu.VMEM((pps // pages_per_block, KV_WIDTH, MOBA_BLOCK), BF16)]),
        compiler_params=pltpu.CompilerParams(dimension_semantics=("arbitrary", "arbitrary"),
                                             vmem_limit_bytes=VMEM_LIMIT_BYTES),
        name="sample_attn",
    )(page_table, qbd, k_new, v_new, *([cache_k] * pps), *([cache_v] * pps))


def _sample_tail_kernel(att_ref, sga_ref, gmg_ref, x_ref, wout_ref, gpost_ref, y_ref):
    y_ref[...] = _tail(att_ref[...], sga_ref, gmg_ref, x_ref, wout_ref, gpost_ref)


def _sample_tail(att, sga, gmg, x, w_out16, g_post):
    return pl.pallas_call(
        _sample_tail_kernel,
        out_shape=jax.ShapeDtypeStruct(x.shape, F32),
        compiler_params=pltpu.CompilerParams(vmem_limit_bytes=VMEM_LIMIT_BYTES),
        name="sample_tail",
    )(att, sga, gmg, x, w_out16, g_post)


def _pair_cat(w):
    g, c, _ = w.shape
    return w.reshape(g // 2, 2, c, c).transpose(0, 2, 1, 3).reshape(g // 2, c, 2 * c)


def _layer(x_p, x_s, cache_k, cache_v, page_table, g_pre, w_in, g_sgu, w_sp, b_sp, w_out, g_post):
    batch, seq, d_model = x_p.shape
    db, t_new, _ = x_s.shape
    assert seq % MOBA_BLOCK == 0 and GM_CHUNK % t_new == 0
    w_in16 = w_in.astype(BF16)
    w_out16 = w_out.astype(BF16)
    g_pre2, g_sgu2, g_post2 = g_pre[None, :], g_sgu[None, :], g_post[None, :]
    grp = jnp.arange(GM_WIDTH) // GM_GROUP_DIM
    gavg = jnp.where(grp[:, None] == grp[None, :], 1.0 / GM_GROUP_DIM, 0.0).astype(BF16)
    bias_p = jnp.repeat(b_sp.T, GM_GROUP_DIM, axis=1)
    wcat_p = _pair_cat(w_sp)
    seqs = GM_CHUNK // t_new
    w_s = jnp.einsum('ab,gts->gatbs', jnp.eye(seqs, dtype=F32), w_sp[:, :t_new, :t_new])
    wcat_s = _pair_cat(w_s.reshape(GM_GROUPS, GM_CHUNK, GM_CHUNK))
    bias_s = jnp.tile(bias_p[:t_new], (seqs, 1))

    xp2 = x_p.reshape(batch * seq, d_model)
    qpad, kt_p, vt_p, kb, vt, kmean, sga_p, gmg_p = _in_proj(
        xp2, g_pre2, w_in16, g_sgu2, gavg, wcat_p, bias_p, rows=PROMPT_ROWS, sample=False, seq=seq)
    y_p = _prompt_attn(qpad, kb, vt, kmean, sga_p, gmg_p, xp2, w_out16, g_post2, batch=batch, seq=seq)

    xs2 = x_s.reshape(db * t_new, d_model)
    q_s, k_s, v_s, sga_s, gmg_s, vn_s = _in_proj(
        xs2, g_pre2, w_in16, g_sgu2, gavg, wcat_s, bias_s, rows=db * t_new, sample=True)
    q5 = q_s.reshape(db, t_new, N_KV_HEADS, KV_GROUP, HEAD_DIM)
    eye = jnp.eye(N_KV_HEADS, dtype=F32)[None, None, :, None, :, None]
    qbd = (q5[:, :, :, :, None, :] * eye).reshape(db, t_new * N_HEADS, KV_WIDTH)
    n_phys, page_size = cache_k.shape[:2]
    ck = cache_k.transpose(0, 2, 3, 1).reshape(n_phys, KV_WIDTH, page_size)
    cv = cache_v.transpose(0, 2, 3, 1).reshape(n_phys, KV_WIDTH, page_size)
    o_full = _sample_attn(page_table, qbd, k_s.reshape(db, t_new, KV_WIDTH), v_s.reshape(db, t_new, KV_WIDTH), ck, cv)
    o6 = o_full.reshape(db, t_new, N_KV_HEADS, KV_GROUP, N_KV_HEADS, HEAD_DIM)
    att_s = jnp.stack([o6[:, :, kh, :, kh, :] for kh in range(N_KV_HEADS)], axis=2)
    att_s = att_s.reshape(db * t_new, ATT_WIDTH)
    y_s = _sample_tail(att_s, sga_s, gmg_s, xs2, w_out16, g_post2)

    rows_of = lambda t: t.reshape(batch, N_KV_HEADS, HEAD_DIM, seq).transpose(0, 3, 1, 2)
    kv_s = (db, t_new, N_KV_HEADS, HEAD_DIM)
    return (y_p.reshape(batch, seq, d_model), y_s.reshape(db, t_new, d_model),
            rows_of(kt_p), rows_of(vt_p), k_s.reshape(kv_s), v_s.reshape(kv_s),
            vn_s.reshape(db, t_new, GM_WIDTH))


def kernel(x_prompt, x_sample, cache_k, cache_v, page_table, g_pre, w_in, g_sgu, w_spatial, b_spatial, w_out, g_post):
    yp, ys = x_prompt, x_sample
    outs = []
    for l in range(w_in.shape[0]):
        yp, ys, kp, vp, kn, vn, sv = _layer(yp, ys, cache_k[l], cache_v[l], page_table, g_pre[l], w_in[l],
                                            g_sgu[l], w_spatial[l], b_spatial[l], w_out[l], g_post[l])
        outs.append((kp, vp, kn, vn, sv))
    stacked = [jnp.stack(parts) for parts in zip(*outs)]
    return (yp, ys, *stacked)
```

```python
import functools

import numpy as np
import jax
import jax.numpy as jnp
from jax import lax
from jax.experimental import pallas as pl
from jax.experimental.pallas import tpu as pltpu

N_HEADS = 8
HEAD_DIM = 64
N_KV_HEADS = 4
KV_GROUP = N_HEADS // N_KV_HEADS
ATT_WIDTH = N_HEADS * HEAD_DIM
KV_WIDTH = N_KV_HEADS * HEAD_DIM
MOBA_BLOCK = 256
MOBA_TOP_K = 3
GM_GROUPS = 8
GM_GROUP_DIM = 64
GM_WIDTH = GM_GROUPS * GM_GROUP_DIM
GM_CHUNK = 128
NORM_EPS = 1e-6
LN_EPS = 1e-5

LANES = 128
MXU_COLS = 256
VMEM_LIMIT_BYTES = 56 * 1024 * 1024
PROMPT_ROWS = 512
L_ROWS = 16

F32 = jnp.float32
BF16 = jnp.bfloat16
NEG_INF = float("-inf")
Q_SCALE = np.float32(HEAD_DIM ** -0.5 * np.log2(np.e))


def _dot(a, b):
    return jnp.dot(a, b, preferred_element_type=F32)


def _dot_nt(a, b):
    return lax.dot_general(a, b, (((1,), (1,)), ((), ())), preferred_element_type=F32)


def _gelu_tanh(x):
    c = np.float32(np.sqrt(2.0 / np.pi))
    return x * (0.5 * (1.0 + jnp.tanh(c * (x + np.float32(0.044715) * (x * x * x)))))


def _silu(x):
    return x / (1.0 + jnp.exp(-x))


def _rms_norm(x, g):
    ms = jnp.mean(x * x, axis=-1, keepdims=True)
    return x * lax.rsqrt(ms + NORM_EPS) * g


def _in_proj_kernel(x_ref, gpre_ref, win_ref, gsgu_ref, gavg_ref, wcat_ref, bias_ref, *out_refs, sample):
    if sample:
        q_ref, k_ref, v_ref, sga_ref, gmg_ref, vn_ref = out_refs
    else:
        q_ref, k_ref, v_ref, kb_ref, vt_ref, kmean_ref, sga_ref, gmg_ref = out_refs
    rows = x_ref.shape[0]
    h = _rms_norm(x_ref[...], gpre_ref[...]).astype(BF16)

    def proj(lo, width):
        return _dot(h, win_ref[:, lo:lo + width])

    o_k = ATT_WIDTH
    o_v = o_k + KV_WIDTH
    o_ga = o_v + KV_WIDTH
    o_u = o_ga + ATT_WIDTH
    o_vs = o_u + GM_WIDTH
    o_gg = o_vs + GM_WIDTH

    zq = proj(0, ATT_WIDTH) * Q_SCALE
    if sample:
        q_ref[...] = zq
    else:
        first_half = lax.broadcasted_iota(jnp.int32, (rows, LANES), 1) < HEAD_DIM
        for kh in range(N_KV_HEADS):
            tile = zq[:, kh * LANES:(kh + 1) * LANES]
            swapped = pltpu.roll(tile, HEAD_DIM, axis=1)
            keep = first_half if kh % 2 == 0 else jnp.logical_not(first_half)
            even_src, odd_src = (tile, swapped) if kh % 2 == 0 else (swapped, tile)
            h0 = 2 * kh
            q_ref[:, h0 * LANES:(h0 + 1) * LANES] = jnp.where(keep, even_src, 0.0).astype(BF16)
            q_ref[:, (h0 + 1) * LANES:(h0 + 2) * LANES] = jnp.where(keep, odd_src, 0.0).astype(BF16)

    zk = proj(o_k, KV_WIDTH)
    zv = proj(o_v, KV_WIDTH)
    if sample:
        k_ref[...] = zk
        v_ref[...] = zv
    else:
        k_ref[...] = zk.T
        zvt = zv.T
        v_ref[...] = zvt
        kb_ref[...] = zk.astype(BF16)
        for c in range(rows // MOBA_BLOCK):
            blk = slice(c * MOBA_BLOCK, (c + 1) * MOBA_BLOCK)
            kmean_ref[c] = jnp.sum(zk[blk], axis=0, keepdims=True) * np.float32(1.0 / MOBA_BLOCK)
            vt_ref[c] = zvt[:, blk].astype(BF16)

    sga_ref[...] = _silu(proj(o_ga, ATT_WIDTH)).astype(BF16)

    u = _gelu_tanh(proj(o_u, GM_WIDTH))
    gv = _gelu_tanh(proj(o_vs, GM_WIDTH))
    def group_mean(t):
        t16 = t.astype(BF16)
        return jnp.concatenate([_dot(t16[:, c:c + MXU_COLS], gavg_ref[...])
                                for c in range(0, GM_WIDTH, MXU_COLS)], axis=1)

    mu = group_mean(gv)
    d = gv - mu
    var = group_mean(d * d)
    vn = d * lax.rsqrt(var + LN_EPS) * gsgu_ref[...]
    if sample:
        vn_ref[...] = vn

    r_i = lax.broadcasted_iota(jnp.int32, (GM_CHUNK, 2 * GM_CHUNK), 0)
    c_i = lax.broadcasted_iota(jnp.int32, (GM_CHUNK, 2 * GM_CHUNK), 1)
    tril = r_i >= (c_i & (GM_CHUNK - 1))
    lane_lo = lax.broadcasted_iota(jnp.int32, (GM_CHUNK, LANES), 1) < GM_GROUP_DIM
    sgg = _silu(proj(o_gg, GM_WIDTH))
    n_chunks = rows // GM_CHUNK
    chunks_per_dot = 2 if n_chunks % 2 == 0 else 1
    for p in range(GM_GROUPS // 2):
        w_pair = jnp.where(tril, wcat_ref[p], 0.0).astype(BF16)
        cols = slice(p * LANES, (p + 1) * LANES)

        def split_groups(c):
            vp = vn[c * GM_CHUNK:(c + 1) * GM_CHUNK, cols]
            return jnp.concatenate([jnp.where(lane_lo, vp, 0.0).astype(BF16),
                                    jnp.where(lane_lo, 0.0, vp).astype(BF16)], axis=0)

        for c0 in range(0, n_chunks, chunks_per_dot):
            cs = range(c0, c0 + chunks_per_dot)
            mixed_all = _dot(w_pair, jnp.concatenate([split_groups(c) for c in cs], axis=1))
            for k, c in enumerate(cs):
                rws = slice(c * GM_CHUNK, (c + 1) * GM_CHUNK)
                mixed = mixed_all[:, k * LANES:(k + 1) * LANES] + bias_ref[:, cols]
                gmg_ref[rws, cols] = (u[rws, cols] * mixed * sgg[rws, cols]).astype(BF16)


def _in_proj(x, g_pre, w_in16, g_sgu, gavg, wcat, bias, *, rows, sample, seq=None):
    n, d_model = x.shape
    assert n % rows == 0 and rows % GM_CHUNK == 0
    steps = n // rows
    row_spec = lambda width: pl.BlockSpec((rows, width), lambda i: (i, 0))
    full = lambda a: pl.BlockSpec(a.shape, lambda i: (0,) * a.ndim)
    if sample:
        out_shape = [jax.ShapeDtypeStruct((n, ATT_WIDTH), F32),
                     jax.ShapeDtypeStruct((n, KV_WIDTH), F32),
                     jax.ShapeDtypeStruct((n, KV_WIDTH), F32),
                     jax.ShapeDtypeStruct((n, ATT_WIDTH), BF16),
                     jax.ShapeDtypeStruct((n, GM_WIDTH), BF16),
                     jax.ShapeDtypeStruct((n, GM_WIDTH), F32)]
        out_specs = [row_spec(ATT_WIDTH), row_spec(KV_WIDTH), row_spec(KV_WIDTH),
                     row_spec(ATT_WIDTH), row_spec(GM_WIDTH), row_spec(GM_WIDTH)]
    else:
        assert rows % MOBA_BLOCK == 0 and seq % rows == 0
        bps = rows // MOBA_BLOCK
        nblk = n // MOBA_BLOCK
        spb = seq // rows
        kv_t_spec = pl.BlockSpec((None, KV_WIDTH, rows), lambda i: (i // spb, 0, i % spb))
        out_shape = [jax.ShapeDtypeStruct((n, N_HEADS * LANES), BF16),
                     jax.ShapeDtypeStruct((n // seq, KV_WIDTH, seq), F32),
                     jax.ShapeDtypeStruct((n // seq, KV_WIDTH, seq), F32),
                     jax.ShapeDtypeStruct((n, KV_WIDTH), BF16),
                     jax.ShapeDtypeStruct((nblk, KV_WIDTH, MOBA_BLOCK), BF16),
                     jax.ShapeDtypeStruct((nblk, 1, KV_WIDTH), F32),
                     jax.ShapeDtypeStruct((n, ATT_WIDTH), BF16),
                     jax.ShapeDtypeStruct((n, GM_WIDTH), BF16)]
        out_specs = [row_spec(N_HEADS * LANES), kv_t_spec, kv_t_spec, row_spec(KV_WIDTH),
                     pl.BlockSpec((bps, KV_WIDTH, MOBA_BLOCK), lambda i: (i, 0, 0)),
                     pl.BlockSpec((bps, 1, KV_WIDTH), lambda i: (i, 0, 0)),
                     row_spec(ATT_WIDTH), row_spec(GM_WIDTH)]
    return pl.pallas_call(
        functools.partial(_in_proj_kernel, sample=sample),
        out_shape=out_shape,
        grid=(steps,),
        in_specs=[row_spec(d_model), full(g_pre), full(w_in16), full(g_sgu), full(gavg), full(wcat), full(bias)],
        out_specs=out_specs,
        compiler_params=pltpu.CompilerParams(dimension_semantics=("arbitrary",),
                                             vmem_limit_bytes=VMEM_LIMIT_BYTES),
        name="in_proj_sample" if sample else "in_proj_prompt",
    )(x, g_pre, w_in16, g_sgu, gavg, wcat, bias)


def _tail(att, sga_ref, gmg_ref, x_ref, wout_ref, gpost_ref):
    mix = jnp.concatenate([(att * sga_ref[...].astype(F32)).astype(BF16), gmg_ref[...]], axis=-1)
    out = _dot(mix, wout_ref[...])
    return x_ref[...] + _rms_norm(out, gpost_ref[...])


def _attn_kernel(pt_ref, q_ref, kb_ref, vt_ref, kmean_ref, sga_ref, gmg_ref, x_ref, wout_ref, gpost_ref,
                 qbd_ref, knew_ref, vnew_ref, *refs, pages_per_step, steps_per_seq, n_past_blocks, page_size):
    k_pages = refs[:pages_per_step]
    v_pages = refs[pages_per_step:2 * pages_per_step]
    (y_ref, sout_ref, sel_ref, g_ref, m_ref, acc_ref, attt_ref, sa_ref, sb_ref, bma_ref, bmb_ref,
     so_ref, sm_ref, sl_ref, sg_ref, kt16_ref, vt16_ref) = refs[2 * pages_per_step:]
    del pt_ref
    i = pl.program_id(1)

    step = pl.program_id(0) * pl.num_programs(1) + i
    _sample_step(qbd_ref, knew_ref, vnew_ref, k_pages, v_pages, sout_ref, so_ref, sm_ref, sl_ref, sg_ref,
                 kt16_ref, vt16_ref, s_idx=lax.rem(step, jnp.int32(steps_per_seq)), n_steps=steps_per_seq,
                 n_past_blocks=n_past_blocks, page_size=page_size)

    nblk = kb_ref.shape[0]
    tq = q_ref.shape[0]
    ones_rows = jnp.ones((L_ROWS, MOBA_BLOCK), BF16)

    def q_head(h):
        return q_ref[:, h * LANES:(h + 1) * LANES]

    def k_tile(j, h):
        a = (h // KV_GROUP) // 2
        return kb_ref[j, :, a * LANES:(a + 1) * LANES]

    def vt_aug(j, h):
        kh = h // KV_GROUP
        return jnp.concatenate([vt_ref[j, kh * HEAD_DIM:(kh + 1) * HEAD_DIM, :], ones_rows], axis=0)

    blk_i = lax.broadcasted_iota(jnp.int32, (nblk, tq), 0)
    km = kmean_ref[...]
    for h in range(N_HEADS):
        a = (h // KV_GROUP) // 2
        km_a = km[:, a * LANES:(a + 1) * LANES]
        km_hi = km_a.astype(BF16)
        km_lo = (km_a - km_hi.astype(F32)).astype(BF16)
        g2 = _dot_nt(jnp.concatenate([km_hi, km_lo], axis=0), q_head(h))
        g_ref[h] = g2[:nblk] + g2[nblk:]

    def count_beaten(n2, ranks):
        tie_wins = blk_i > n2
        out = []
        for h in range(N_HEADS):
            g = g_ref[h]
            gb = g_ref[h, pl.ds(n2, 1), :]
            out.append(ranks[h] + jnp.where(tie_wins, jnp.where(g > gb, 0.0, 1.0), jnp.where(gb > g, 1.0, 0.0)))
        return tuple(out)

    ranks = lax.fori_loop(0, i, count_beaten, tuple(jnp.zeros((nblk, tq), F32) for _ in range(N_HEADS)))
    for h in range(N_HEADS):
        sel_ref[h] = jnp.where((blk_i < i) & (ranks[h] < MOBA_TOP_K), 1.0, 0.0)

    causal = (lax.broadcasted_iota(jnp.int32, (MOBA_BLOCK, tq), 0)
              <= lax.broadcasted_iota(jnp.int32, (MOBA_BLOCK, tq), 1))

    head_pairs = [(KV_GROUP * kh, KV_GROUP * kh + 1) for kh in range(N_KV_HEADS)]
    bufs = ((sa_ref, bma_ref), (sb_ref, bmb_ref))

    def scores(j, mask, buf, heads):
        s_buf, bm_buf = buf
        for h in heads:
            s = _dot_nt(k_tile(j, h), q_head(h))
            if mask is not None:
                s = jnp.where(mask, s, NEG_INF)
            s16 = s.astype(BF16)
            s_buf[h] = s16
            bm_buf[h] = jnp.max(s16, axis=0, keepdims=True).astype(F32)

    def attend_own(buf, heads):
        s_buf, bm_buf = buf
        for h in heads:
            m = bm_buf[h]
            m_ref[h] = m
            acc_ref[h] = _dot(vt_aug(i, h), jnp.exp2(s_buf[h] - m.astype(BF16)))

    def attend(j, buf, heads):
        s_buf, bm_buf = buf
        for h in heads:
            chosen = sel_ref[h, pl.ds(j, 1), :] > 0.5
            m_old = m_ref[h]
            m_new = jnp.where(chosen, jnp.maximum(m_old, bm_buf[h]), m_old)
            p = jnp.exp2(s_buf[h] - jnp.where(chosen, m_new, jnp.inf).astype(BF16))
            acc_ref[h] = acc_ref[h] * jnp.exp2(m_old - m_new) + _dot(vt_aug(j, h), p)
            m_ref[h] = m_new

    def overlapped(attend_pair, j_next, buf_next):
        scores(j_next, None, buf_next, head_pairs[0])
        for k, pair in enumerate(head_pairs):
            attend_pair(pair)
            if k + 1 < len(head_pairs):
                scores(j_next, None, buf_next, head_pairs[k + 1])

    scores(i, causal, bufs[0], range(N_HEADS))
    overlapped(lambda pair: attend_own(bufs[0], pair), 0, bufs[1])

    def past_pair(t, carry):
        j0 = 2 * t
        overlapped(lambda pair: attend(j0, bufs[1], pair), j0 + 1, bufs[0])
        overlapped(lambda pair: attend(j0 + 1, bufs[0], pair), j0 + 2, bufs[1])
        return carry

    lax.fori_loop(0, lax.shift_right_logical(i, jnp.int32(1)), past_pair, 0)

    @pl.when((i & 1) == 1)
    def _():
        attend(i - 1, bufs[1], range(N_HEADS))

    for h in range(N_HEADS):
        acc = acc_ref[h]
        attt_ref[h * HEAD_DIM:(h + 1) * HEAD_DIM, :] = acc[:HEAD_DIM] / acc[HEAD_DIM:HEAD_DIM + 1]
    y_ref[...] = _tail(attt_ref[...].T, sga_ref, gmg_ref, x_ref, wout_ref, gpost_ref)


def _attn(page_table, qpad, kb, vt, kmean, sga, gmg, x, w_out16, g_post, qbd, k_new, v_new, cache_k, cache_v,
          *, batch, seq):
    nblk = seq // MOBA_BLOCK
    d_model = x.shape[-1]
    steps = batch * nblk
    db, n_pages = page_table.shape
    page_size = cache_k.shape[2]
    nq = qbd.shape[1]
    t_new = k_new.shape[1]
    pages_per_block = MOBA_BLOCK // page_size
    n_past_blocks = n_pages // pages_per_block
    assert (db * n_pages) % steps == 0
    pps = db * n_pages // steps
    assert n_pages % pps == 0 and pps % pages_per_block == 0
    spq = n_pages // pps
    assert n_past_blocks <= LANES and t_new <= LANES

    tile = lambda width: pl.BlockSpec((MOBA_BLOCK, width), lambda b, i, pt: (b * nblk + i, 0))
    per_batch = lambda r, c: pl.BlockSpec((None, nblk, r, c), lambda b, i, pt: (b, 0, 0, 0))
    full = lambda a: pl.BlockSpec(a.shape, lambda b, i, pt: (0,) * a.ndim)
    per_seq = lambda rows: pl.BlockSpec((None, rows, KV_WIDTH), lambda b, i, pt: ((b * nblk + i) // spq, 0, 0))

    def page_spec(r):
        def index(b, i, pt):
            g = b * nblk + i
            return (pt[g // spq, (g % spq) * pps + r], 0, 0)
        return pl.BlockSpec((None, KV_WIDTH, page_size), index)

    kernel = functools.partial(_attn_kernel, pages_per_step=pps, steps_per_seq=spq,
                               n_past_blocks=n_past_blocks, page_size=page_size)
    blocks_per_step = pps // pages_per_block
    return pl.pallas_call(
        kernel,
        out_shape=[jax.ShapeDtypeStruct(x.shape, F32), jax.ShapeDtypeStruct((db, nq, KV_WIDTH), F32)],
        grid_spec=pltpu.PrefetchScalarGridSpec(
            num_scalar_prefetch=1,
            grid=(batch, nblk),
            in_specs=([tile(N_HEADS * LANES),
                       per_batch(MOBA_BLOCK, KV_WIDTH),
                       per_batch(KV_WIDTH, MOBA_BLOCK),
                       pl.BlockSpec((None, nblk, KV_WIDTH), lambda b, i, pt: (b, 0, 0)),
                       tile(ATT_WIDTH), tile(GM_WIDTH), tile(d_model), full(w_out16), full(g_post),
                       per_seq(nq), per_seq(t_new), per_seq(t_new)]
                      + [page_spec(r) for r in range(pps)] + [page_spec(r) for r in range(pps)]),
            out_specs=[tile(d_model), per_seq(nq)],
            scratch_shapes=[pltpu.VMEM((N_HEADS, nblk, MOBA_BLOCK), F32),
                            pltpu.VMEM((N_HEADS, nblk, MOBA_BLOCK), F32),
                            pltpu.VMEM((N_HEADS, 1, MOBA_BLOCK), F32),
                            pltpu.VMEM((N_HEADS, HEAD_DIM + L_ROWS, MOBA_BLOCK), F32),
                            pltpu.VMEM((ATT_WIDTH, MOBA_BLOCK), F32),
                            pltpu.VMEM((N_HEADS, MOBA_BLOCK, MOBA_BLOCK), BF16),
                            pltpu.VMEM((N_HEADS, MOBA_BLOCK, MOBA_BLOCK), BF16),
                            pltpu.VMEM((N_HEADS, 1, MOBA_BLOCK), F32),
                            pltpu.VMEM((N_HEADS, 1, MOBA_BLOCK), F32),
                            pltpu.VMEM((n_past_blocks, nq, KV_WIDTH), F32),
                            pltpu.VMEM((nq, LANES), F32),
                            pltpu.VMEM((nq, LANES), F32),
                            pltpu.VMEM((nq, LANES), F32),
                            pltpu.VMEM((blocks_per_step, KV_WIDTH, MOBA_BLOCK), BF16),
                            pltpu.VMEM((blocks_per_step, KV_WIDTH, MOBA_BLOCK), BF16)]),
        compiler_params=pltpu.CompilerParams(dimension_semantics=("arbitrary", "arbitrary"),
                                             vmem_limit_bytes=VMEM_LIMIT_BYTES),
        name="attn",
    )(page_table, qpad, kb.reshape(batch, nblk, MOBA_BLOCK, KV_WIDTH), vt.reshape(batch, nblk, KV_WIDTH, MOBA_BLOCK),
      kmean.reshape(batch, nblk, KV_WIDTH), sga, gmg, x, w_out16, g_post, qbd, k_new, v_new,
      *([cache_k] * pps), *([cache_v] * pps))


def _sample_step(qbd_ref, knew_ref, vnew_ref, k_pages, v_pages, out_ref, o_ref, m_ref, l_ref, g_ref,
                 kt16_ref, vt16_ref, *, s_idx, n_steps, n_past_blocks, page_size):
    pages_per_block = MOBA_BLOCK // page_size
    blocks_per_step = len(k_pages) // pages_per_block
    nq = qbd_ref.shape[0]
    t_new = knew_ref.shape[0]
    lane = lax.broadcasted_iota(jnp.int32, (nq, LANES), 1)

    @pl.when(s_idx == 0)
    def _():
        m_ref[...] = jnp.full((nq, LANES), NEG_INF, F32)
        l_ref[...] = jnp.zeros((nq, LANES), F32)
        g_ref[...] = jnp.full((nq, LANES), NEG_INF, F32)

    qbd = qbd_ref[...]
    qbd16 = qbd.astype(BF16)
    for jj in range(blocks_per_step):
        for r in range(pages_per_block):
            cols = slice(r * page_size, (r + 1) * page_size)
            kt16_ref[jj, :, cols] = k_pages[jj * pages_per_block + r][...].astype(BF16)
            vt16_ref[jj, :, cols] = v_pages[jj * pages_per_block + r][...].astype(BF16)
    s_blk = [_dot(qbd16, kt16_ref[jj]) for jj in range(blocks_per_step)]
    m_blk = [jnp.max(s, axis=-1, keepdims=True) for s in s_blk]
    g_blk = [jnp.sum(s, axis=-1, keepdims=True) * np.float32(1.0 / MOBA_BLOCK) for s in s_blk]
    p_blk = [jnp.exp2(s - m) for s, m in zip(s_blk, m_blk)]
    l_blk = [jnp.sum(p, axis=-1, keepdims=True) for p in p_blk]
    m_all, l_all, g_all = m_ref[...], l_ref[...], g_ref[...]
    for jj in range(blocks_per_step):
        blk = s_idx * blocks_per_step + jj
        o_ref[blk] = _dot_nt(p_blk[jj].astype(BF16), vt16_ref[jj])
        hit = lane == blk
        m_all = jnp.where(hit, m_blk[jj], m_all)
        l_all = jnp.where(hit, l_blk[jj], l_all)
        g_all = jnp.where(hit, g_blk[jj], g_all)
    m_ref[...], l_ref[...], g_ref[...] = m_all, l_all, g_all

    @pl.when(s_idx == n_steps - 1)
    def _():
        g = g_ref[...]
        lane_f = lane.astype(F32)
        sel = jnp.zeros((nq, LANES), jnp.bool_)
        for _ in range(min(MOBA_TOP_K, n_past_blocks)):
            best = jnp.max(g, axis=-1, keepdims=True)
            idx = jnp.min(jnp.where(g == best, lane_f, np.float32(LANES)), axis=-1, keepdims=True)
            pick = lane_f == idx
            sel = sel | pick
            g = jnp.where(pick, NEG_INF, g)
        row = lax.broadcasted_iota(jnp.int32, (nq, LANES), 0)
        rows_per_token = nq // t_new
        s_own = jnp.full((nq, LANES), NEG_INF, F32)
        for t in range(t_new):
            s_t = jnp.sum(qbd * knew_ref[t:t + 1, :], axis=-1, keepdims=True)
            s_own = jnp.where((lane == t) & (row >= t * rows_per_token), s_t, s_own)
        m_blocks = m_ref[...]
        m_tot = jnp.maximum(jnp.max(jnp.where(sel, m_blocks, NEG_INF), axis=-1, keepdims=True),
                            jnp.max(s_own, axis=-1, keepdims=True))
        w = jnp.where(sel, jnp.exp2(m_blocks - m_tot), 0.0)
        p_own = jnp.exp2(s_own - m_tot)
        l_tot = (jnp.sum(w * l_ref[...], axis=-1, keepdims=True) + jnp.sum(p_own, axis=-1, keepdims=True))
        acc = jnp.zeros((nq, KV_WIDTH), F32)
        for n in range(n_past_blocks):
            acc = acc + jnp.sum(jnp.where(lane == n, w, 0.0), axis=-1, keepdims=True) * o_ref[n]
        for t in range(t_new):
            acc = acc + jnp.sum(jnp.where(lane == t, p_own, 0.0), axis=-1, keepdims=True) * vnew_ref[t:t + 1, :]
        out_ref[...] = acc / l_tot


def _sample_tail_kernel(att_ref, sga_ref, gmg_ref, x_ref, wout_ref, gpost_ref, y_ref):
    y_ref[...] = _tail(att_ref[...], sga_ref, gmg_ref, x_ref, wout_ref, gpost_ref)


def _sample_tail(att, sga, gmg, x, w_out16, g_post):
    return pl.pallas_call(
        _sample_tail_kernel,
        out_shape=jax.ShapeDtypeStruct(x.shape, F32),
        compiler_params=pltpu.CompilerParams(vmem_limit_bytes=VMEM_LIMIT_BYTES),
        name="sample_tail",
    )(att, sga, gmg, x, w_out16, g_post)


def _pair_cat(w):
    g, c, _ = w.shape
    return w.reshape(g // 2, 2, c, c).transpose(0, 2, 1, 3).reshape(g // 2, c, 2 * c)


def _layer(x_p, x_s, cache_k, cache_v, page_table, g_pre, w_in, g_sgu, w_sp, b_sp, w_out, g_post):
    batch, seq, d_model = x_p.shape
    db, t_new, _ = x_s.shape
    assert seq % MOBA_BLOCK == 0 and GM_CHUNK % t_new == 0
    w_in16 = w_in.astype(BF16)
    w_out16 = w_out.astype(BF16)
    g_pre2, g_sgu2, g_post2 = g_pre[None, :], g_sgu[None, :], g_post[None, :]
    grp = jnp.arange(MXU_COLS) // GM_GROUP_DIM
    gavg = jnp.where(grp[:, None] == grp[None, :], 1.0 / GM_GROUP_DIM, 0.0).astype(BF16)
    bias_p = jnp.repeat(b_sp.T, GM_GROUP_DIM, axis=1)
    wcat_p = _pair_cat(w_sp)
    seqs = GM_CHUNK // t_new
    w_s = jnp.einsum('ab,gts->gatbs', jnp.eye(seqs, dtype=F32), w_sp[:, :t_new, :t_new])
    wcat_s = _pair_cat(w_s.reshape(GM_GROUPS, GM_CHUNK, GM_CHUNK))
    bias_s = jnp.tile(bias_p[:t_new], (seqs, 1))

    xp2 = x_p.reshape(batch * seq, d_model)
    qpad, kt_p, vt_p, kb, vt, kmean, sga_p, gmg_p = _in_proj(
        xp2, g_pre2, w_in16, g_sgu2, gavg, wcat_p, bias_p, rows=PROMPT_ROWS, sample=False, seq=seq)

    xs2 = x_s.reshape(db * t_new, d_model)
    q_s, k_s, v_s, sga_s, gmg_s, vn_s = _in_proj(
        xs2, g_pre2, w_in16, g_sgu2, gavg, wcat_s, bias_s, rows=db * t_new, sample=True)
    q5 = q_s.reshape(db, t_new, N_KV_HEADS, KV_GROUP, HEAD_DIM)
    eye = jnp.eye(N_KV_HEADS, dtype=F32)[None, None, :, None, :, None]
    qbd = (q5[:, :, :, :, None, :] * eye).reshape(db, t_new * N_HEADS, KV_WIDTH)
    n_phys, page_size = cache_k.shape[:2]
    ck = cache_k.transpose(0, 2, 3, 1).reshape(n_phys, KV_WIDTH, page_size)
    cv = cache_v.transpose(0, 2, 3, 1).reshape(n_phys, KV_WIDTH, page_size)
    y_p, o_full = _attn(page_table, qpad, kb, vt, kmean, sga_p, gmg_p, xp2, w_out16, g_post2, qbd,
                        k_s.reshape(db, t_new, KV_WIDTH), v_s.reshape(db, t_new, KV_WIDTH), ck, cv,
                        batch=batch, seq=seq)
    o6 = o_full.reshape(db, t_new, N_KV_HEADS, KV_GROUP, N_KV_HEADS, HEAD_DIM)
    att_s = jnp.stack([o6[:, :, kh, :, kh, :] for kh in range(N_KV_HEADS)], axis=2)
    att_s = att_s.reshape(db * t_new, ATT_WIDTH)
    y_s = _sample_tail(att_s, sga_s, gmg_s, xs2, w_out16, g_post2)

    rows_of = lambda t: t.reshape(batch, N_KV_HEADS, HEAD_DIM, seq).transpose(0, 3, 1, 2)
    kv_s = (db, t_new, N_KV_HEADS, HEAD_DIM)
    return (y_p.reshape(batch, seq, d_model), y_s.reshape(db, t_new, d_model),
            rows_of(kt_p), rows_of(vt_p), k_s.reshape(kv_s), v_s.reshape(kv_s),
            vn_s.reshape(db, t_new, GM_WIDTH))


def kernel(x_prompt, x_sample, cache_k, cache_v, page_table, g_pre, w_in, g_sgu, w_spatial, b_spatial, w_out, g_post):
    yp, ys = x_prompt, x_sample
    outs = []
    for l in range(w_in.shape[0]):
        yp, ys, kp, vp, kn, vn, sv = _layer(yp, ys, cache_k[l], cache_v[l], page_table, g_pre[l], w_in[l],
                                            g_sgu[l], w_spatial[l], b_spatial[l], w_out[l], g_post[l])
        outs.append((kp, vp, kn, vn, sv))
    stacked = [jnp.stack(parts) for parts in zip(*outs)]
    return (yp, ys, *stacked)
```

```python
import functools

import numpy as np
import jax
import jax.numpy as jnp
from jax import lax
from jax.experimental import pallas as pl
from jax.experimental.pallas import tpu as pltpu

N_HEADS = 8
HEAD_DIM = 64
N_KV_HEADS = 4
KV_GROUP = N_HEADS // N_KV_HEADS
ATT_WIDTH = N_HEADS * HEAD_DIM
KV_WIDTH = N_KV_HEADS * HEAD_DIM
MOBA_BLOCK = 256
MOBA_TOP_K = 3
GM_GROUPS = 8
GM_GROUP_DIM = 64
GM_WIDTH = GM_GROUPS * GM_GROUP_DIM
GM_CHUNK = 128
NORM_EPS = 1e-6
LN_EPS = 1e-5

LANES = 128
MXU_COLS = 256
VMEM_LIMIT_BYTES = 56 * 1024 * 1024
PROMPT_ROWS = 512
L_ROWS = 16

F32 = jnp.float32
BF16 = jnp.bfloat16
NEG_INF = float("-inf")
Q_SCALE = np.float32(HEAD_DIM ** -0.5 * np.log2(np.e))


def _dot(a, b):
    return jnp.dot(a, b, preferred_element_type=F32)


def _dot_nt(a, b):
    return lax.dot_general(a, b, (((1,), (1,)), ((), ())), preferred_element_type=F32)


def _gelu_tanh(x):
    c = np.float32(np.sqrt(2.0 / np.pi))
    return x * (0.5 * (1.0 + jnp.tanh(c * (x + np.float32(0.044715) * (x * x * x)))))


def _silu(x):
    return x / (1.0 + jnp.exp(-x))


def _rms_norm(x, g):
    ms = jnp.mean(x * x, axis=-1, keepdims=True)
    return x * lax.rsqrt(ms + NORM_EPS) * g


def _in_proj_kernel(x_ref, gpre_ref, win_ref, gsgu_ref, gavg_ref, wcat_ref, bias_ref, *out_refs, sample):
    if sample:
        q_ref, k_ref, v_ref, sga_ref, gmg_ref, vn_ref = out_refs
    else:
        q_ref, k_ref, v_ref, kb_ref, vt_ref, kmean_ref, sga_ref, gmg_ref = out_refs
    rows = x_ref.shape[0]
    h = _rms_norm(x_ref[...], gpre_ref[...]).astype(BF16)

    def proj(lo, width):
        return _dot(h, win_ref[:, lo:lo + width])

    o_k = ATT_WIDTH
    o_v = o_k + KV_WIDTH
    o_ga = o_v + KV_WIDTH
    o_u = o_ga + ATT_WIDTH
    o_vs = o_u + GM_WIDTH
    o_gg = o_vs + GM_WIDTH

    zq = proj(0, ATT_WIDTH) * Q_SCALE
    if sample:
        q_ref[...] = zq
    else:
        first_half = lax.broadcasted_iota(jnp.int32, (rows, LANES), 1) < HEAD_DIM
        for kh in range(N_KV_HEADS):
            tile = zq[:, kh * LANES:(kh + 1) * LANES]
            swapped = pltpu.roll(tile, HEAD_DIM, axis=1)
            keep = first_half if kh % 2 == 0 else jnp.logical_not(first_half)
            even_src, odd_src = (tile, swapped) if kh % 2 == 0 else (swapped, tile)
            h0 = 2 * kh
            q_ref[:, h0 * LANES:(h0 + 1) * LANES] = jnp.where(keep, even_src, 0.0).astype(BF16)
            q_ref[:, (h0 + 1) * LANES:(h0 + 2) * LANES] = jnp.where(keep, odd_src, 0.0).astype(BF16)

    zk = proj(o_k, KV_WIDTH)
    zv = proj(o_v, KV_WIDTH)
    if sample:
        k_ref[...] = zk
        v_ref[...] = zv
    else:
        k_ref[...] = zk.T
        zvt = zv.T
        v_ref[...] = zvt
        kb_ref[...] = zk.astype(BF16)
        for c in range(rows // MOBA_BLOCK):
            blk = slice(c * MOBA_BLOCK, (c + 1) * MOBA_BLOCK)
            kmean_ref[c] = jnp.sum(zk[blk], axis=0, keepdims=True) * np.float32(1.0 / MOBA_BLOCK)
            vt_ref[c] = zvt[:, blk].astype(BF16)

    sga_ref[...] = _silu(proj(o_ga, ATT_WIDTH)).astype(BF16)

    u = _gelu_tanh(proj(o_u, GM_WIDTH))
    gv = _gelu_tanh(proj(o_vs, GM_WIDTH))
    def group_mean(t):
        t16 = t.astype(BF16)
        return jnp.concatenate([_dot(t16[:, c:c + MXU_COLS], gavg_ref[...])
                                for c in range(0, GM_WIDTH, MXU_COLS)], axis=1)

    mu = group_mean(gv)
    d = gv - mu
    var = group_mean(d * d)
    vn = d * lax.rsqrt(var + LN_EPS) * gsgu_ref[...]
    if sample:
        vn_ref[...] = vn

    r_i = lax.broadcasted_iota(jnp.int32, (GM_CHUNK, 2 * GM_CHUNK), 0)
    c_i = lax.broadcasted_iota(jnp.int32, (GM_CHUNK, 2 * GM_CHUNK), 1)
    tril = r_i >= (c_i & (GM_CHUNK - 1))
    lane_lo = lax.broadcasted_iota(jnp.int32, (GM_CHUNK, LANES), 1) < GM_GROUP_DIM
    sgg = _silu(proj(o_gg, GM_WIDTH))
    n_chunks = rows // GM_CHUNK
    chunks_per_dot = 2 if n_chunks % 2 == 0 else 1
    for p in range(GM_GROUPS // 2):
        w_pair = jnp.where(tril, wcat_ref[p], 0.0).astype(BF16)
        cols = slice(p * LANES, (p + 1) * LANES)

        def split_groups(c):
            vp = vn[c * GM_CHUNK:(c + 1) * GM_CHUNK, cols]
            return jnp.concatenate([jnp.where(lane_lo, vp, 0.0).astype(BF16),
                                    jnp.where(lane_lo, 0.0, vp).astype(BF16)], axis=0)

        for c0 in range(0, n_chunks, chunks_per_dot):
            cs = range(c0, c0 + chunks_per_dot)
            mixed_all = _dot(w_pair, jnp.concatenate([split_groups(c) for c in cs], axis=1))
            for k, c in enumerate(cs):
                rws = slice(c * GM_CHUNK, (c + 1) * GM_CHUNK)
                mixed = mixed_all[:, k * LANES:(k + 1) * LANES] + bias_ref[:, cols]
                gmg_ref[rws, cols] = (u[rws, cols] * mixed * sgg[rws, cols]).astype(BF16)


def _in_proj(x, g_pre, w_in16, g_sgu, gavg, wcat, bias, *, rows, sample, seq=None):
    n, d_model = x.shape
    assert n % rows == 0 and rows % GM_CHUNK == 0
    steps = n // rows
    row_spec = lambda width: pl.BlockSpec((rows, width), lambda i: (i, 0))
    full = lambda a: pl.BlockSpec(a.shape, lambda i: (0,) * a.ndim)
    if sample:
        out_shape = [jax.ShapeDtypeStruct((n, ATT_WIDTH), F32),
                     jax.ShapeDtypeStruct((n, KV_WIDTH), F32),
                     jax.ShapeDtypeStruct((n, KV_WIDTH), F32),
                     jax.ShapeDtypeStruct((n, ATT_WIDTH), BF16),
                     jax.ShapeDtypeStruct((n, GM_WIDTH), BF16),
                     jax.ShapeDtypeStruct((n, GM_WIDTH), F32)]
        out_specs = [row_spec(ATT_WIDTH), row_spec(KV_WIDTH), row_spec(KV_WIDTH),
                     row_spec(ATT_WIDTH), row_spec(GM_WIDTH), row_spec(GM_WIDTH)]
    else:
        assert rows % MOBA_BLOCK == 0 and seq % rows == 0
        bps = rows // MOBA_BLOCK
        nblk = n // MOBA_BLOCK
        spb = seq // rows
        kv_t_spec = pl.BlockSpec((None, KV_WIDTH, rows), lambda i: (i // spb, 0, i % spb))
        out_shape = [jax.ShapeDtypeStruct((n, N_HEADS * LANES), BF16),
                     jax.ShapeDtypeStruct((n // seq, KV_WIDTH, seq), F32),
                     jax.ShapeDtypeStruct((n // seq, KV_WIDTH, seq), F32),
                     jax.ShapeDtypeStruct((n, KV_WIDTH), BF16),
                     jax.ShapeDtypeStruct((nblk, KV_WIDTH, MOBA_BLOCK), BF16),
                     jax.ShapeDtypeStruct((nblk, 1, KV_WIDTH), F32),
                     jax.ShapeDtypeStruct((n, ATT_WIDTH), BF16),
                     jax.ShapeDtypeStruct((n, GM_WIDTH), BF16)]
        out_specs = [row_spec(N_HEADS * LANES), kv_t_spec, kv_t_spec, row_spec(KV_WIDTH),
                     pl.BlockSpec((bps, KV_WIDTH, MOBA_BLOCK), lambda i: (i, 0, 0)),
                     pl.BlockSpec((bps, 1, KV_WIDTH), lambda i: (i, 0, 0)),
                     row_spec(ATT_WIDTH), row_spec(GM_WIDTH)]
    return pl.pallas_call(
        functools.partial(_in_proj_kernel, sample=sample),
        out_shape=out_shape,
        grid=(steps,),
        in_specs=[row_spec(d_model), full(g_pre), full(w_in16), full(g_sgu), full(gavg), full(wcat), full(bias)],
        out_specs=out_specs,
        compiler_params=pltpu.CompilerParams(dimension_semantics=("arbitrary",),
                                             vmem_limit_bytes=VMEM_LIMIT_BYTES),
        name="in_proj_sample" if sample else "in_proj_prompt",
    )(x, g_pre, w_in16, g_sgu, gavg, wcat, bias)


def _tail(att, sga_ref, gmg_ref, x_ref, wout_ref, gpost_ref):
    mix = jnp.concatenate([(att * sga_ref[...].astype(F32)).astype(BF16), gmg_ref[...]], axis=-1)
    out = _dot(mix, wout_ref[...])
    return x_ref[...] + _rms_norm(out, gpost_ref[...])


def _attn_kernel(pt_ref, q_ref, kb_ref, vt_ref, kmean_ref, sga_ref, gmg_ref, x_ref, wout_ref, gpost_ref,
                 qbd_ref, knew_ref, vnew_ref, ck_hbm, cv_hbm,
                 y_ref, sout_ref, sel_ref, g_ref, m_ref, acc_ref, attt_ref, sa_ref, sb_ref, bma_ref, bmb_ref,
                 so_ref, sm_ref, sl_ref, sg_ref, kt16_ref, vt16_ref, kbuf_ref, vbuf_ref, page_sem,
                 *, pages_per_step, steps_per_seq, n_past_blocks):
    i = pl.program_id(1)
    page_size = kbuf_ref.shape[-1]

    step = pl.program_id(0) * pl.num_programs(1) + i
    n_steps = pl.num_programs(0) * pl.num_programs(1)
    slot = step & 1

    def page_copies(for_step, into_slot):
        seq = for_step // steps_per_seq
        first = lax.rem(for_step, jnp.int32(steps_per_seq)) * pages_per_step
        copies = []
        for r in range(pages_per_step):
            page = pt_ref[seq, first + r]
            copies.append(pltpu.make_async_copy(ck_hbm.at[page], kbuf_ref.at[into_slot, r], page_sem.at[into_slot, 0]))
            copies.append(pltpu.make_async_copy(cv_hbm.at[page], vbuf_ref.at[into_slot, r], page_sem.at[into_slot, 1]))
        return copies

    @pl.when(step == 0)
    def _():
        for c in page_copies(step, slot):
            c.start()

    for c in page_copies(step, slot):
        c.wait()
    next_step = jnp.where(step + 1 < n_steps, step + 1, 0)
    for c in page_copies(next_step, 1 - slot):
        c.start()
    _sample_step(qbd_ref, knew_ref, vnew_ref,
                 [kbuf_ref.at[slot, r] for r in range(pages_per_step)],
                 [vbuf_ref.at[slot, r] for r in range(pages_per_step)],
                 sout_ref, so_ref, sm_ref, sl_ref, sg_ref, kt16_ref, vt16_ref,
                 s_idx=lax.rem(step, jnp.int32(steps_per_seq)), n_steps=steps_per_seq,
                 n_past_blocks=n_past_blocks, page_size=page_size)

    @pl.when(step == n_steps - 1)
    def _():
        for c in page_copies(next_step, 1 - slot):
            c.wait()

    nblk = kb_ref.shape[0]
    tq = q_ref.shape[0]
    ones_rows = jnp.ones((L_ROWS, MOBA_BLOCK), BF16)

    def q_head(h):
        return q_ref[:, h * LANES:(h + 1) * LANES]

    def k_tile(j, h):
        a = (h // KV_GROUP) // 2
        return kb_ref[j, :, a * LANES:(a + 1) * LANES]

    def vt_aug(j, h):
        kh = h // KV_GROUP
        return jnp.concatenate([vt_ref[j, kh * HEAD_DIM:(kh + 1) * HEAD_DIM, :], ones_rows], axis=0)

    blk_i = lax.broadcasted_iota(jnp.int32, (nblk, tq), 0)
    km = kmean_ref[...]
    for h in range(N_HEADS):
        a = (h // KV_GROUP) // 2
        km_a = km[:, a * LANES:(a + 1) * LANES]
        km_hi = km_a.astype(BF16)
        km_lo = (km_a - km_hi.astype(F32)).astype(BF16)
        g2 = _dot_nt(jnp.concatenate([km_hi, km_lo], axis=0), q_head(h))
        g_ref[h] = g2[:nblk] + g2[nblk:]

    def count_beaten(n2, ranks):
        tie_wins = blk_i > n2
        out = []
        for h in range(N_HEADS):
            g = g_ref[h]
            gb = g_ref[h, pl.ds(n2, 1), :]
            out.append(ranks[h] + jnp.where(tie_wins, jnp.where(g > gb, 0.0, 1.0), jnp.where(gb > g, 1.0, 0.0)))
        return tuple(out)

    ranks = lax.fori_loop(0, i, count_beaten, tuple(jnp.zeros((nblk, tq), F32) for _ in range(N_HEADS)))
    for h in range(N_HEADS):
        sel_ref[h] = jnp.where((blk_i < i) & (ranks[h] < MOBA_TOP_K), 1.0, 0.0)

    causal = (lax.broadcasted_iota(jnp.int32, (MOBA_BLOCK, tq), 0)
              <= lax.broadcasted_iota(jnp.int32, (MOBA_BLOCK, tq), 1))

    head_pairs = [(KV_GROUP * kh, KV_GROUP * kh + 1) for kh in range(N_KV_HEADS)]
    bufs = ((sa_ref, bma_ref), (sb_ref, bmb_ref))

    def scores(j, mask, buf, heads):
        s_buf, bm_buf = buf
        for h in heads:
            s = _dot_nt(k_tile(j, h), q_head(h))
            if mask is not None:
                s = jnp.where(mask, s, NEG_INF)
            s16 = s.astype(BF16)
            s_buf[h] = s16
            bm_buf[h] = jnp.max(s16, axis=0, keepdims=True).astype(F32)

    def attend_own(buf, heads):
        s_buf, bm_buf = buf
        for h in heads:
            m = bm_buf[h]
            m_ref[h] = m
            acc_ref[h] = _dot(vt_aug(i, h), jnp.exp2(s_buf[h] - m.astype(BF16)))

    def attend(j, buf, heads):
        s_buf, bm_buf = buf
        for h in heads:
            chosen = sel_ref[h, pl.ds(j, 1), :] > 0.5
            m_old = m_ref[h]
            m_new = jnp.where(chosen, jnp.maximum(m_old, bm_buf[h]), m_old)
            p = jnp.exp2(s_buf[h] - jnp.where(chosen, m_new, jnp.inf).astype(BF16))
            acc_ref[h] = acc_ref[h] * jnp.exp2(m_old - m_new) + _dot(vt_aug(j, h), p)
            m_ref[h] = m_new

    def overlapped(attend_pair, j_next, buf_next):
        scores(j_next, None, buf_next, head_pairs[0])
        for k, pair in enumerate(head_pairs):
            attend_pair(pair)
            if k + 1 < len(head_pairs):
                scores(j_next, None, buf_next, head_pairs[k + 1])

    scores(i, causal, bufs[0], range(N_HEADS))
    overlapped(lambda pair: attend_own(bufs[0], pair), 0, bufs[1])

    def past_pair(t, carry):
        j0 = 2 * t
        overlapped(lambda pair: attend(j0, bufs[1], pair), j0 + 1, bufs[0])
        overlapped(lambda pair: attend(j0 + 1, bufs[0], pair), j0 + 2, bufs[1])
        return carry

    lax.fori_loop(0, lax.shift_right_logical(i, jnp.int32(1)), past_pair, 0)

    @pl.when((i & 1) == 1)
    def _():
        attend(i - 1, bufs[1], range(N_HEADS))

    for h in range(N_HEADS):
        acc = acc_ref[h]
        attt_ref[h * HEAD_DIM:(h + 1) * HEAD_DIM, :] = acc[:HEAD_DIM] / acc[HEAD_DIM:HEAD_DIM + 1]
    y_ref[...] = _tail(attt_ref[...].T, sga_ref, gmg_ref, x_ref, wout_ref, gpost_ref)


def _attn(page_table, qpad, kb, vt, kmean, sga, gmg, x, w_out16, g_post, qbd, k_new, v_new, cache_k, cache_v,
          *, batch, seq):
    nblk = seq // MOBA_BLOCK
    d_model = x.shape[-1]
    steps = batch * nblk
    db, n_pages = page_table.shape
    page_size = cache_k.shape[2]
    nq = qbd.shape[1]
    t_new = k_new.shape[1]
    pages_per_block = MOBA_BLOCK // page_size
    n_past_blocks = n_pages // pages_per_block
    assert (db * n_pages) % steps == 0
    pps = db * n_pages // steps
    assert n_pages % pps == 0 and pps % pages_per_block == 0
    spq = n_pages // pps
    assert n_past_blocks <= LANES and t_new <= LANES

    tile = lambda width: pl.BlockSpec((MOBA_BLOCK, width), lambda b, i, pt: (b * nblk + i, 0))
    per_batch = lambda r, c: pl.BlockSpec((None, nblk, r, c), lambda b, i, pt: (b, 0, 0, 0))
    full = lambda a: pl.BlockSpec(a.shape, lambda b, i, pt: (0,) * a.ndim)
    per_seq = lambda rows: pl.BlockSpec((None, rows, KV_WIDTH), lambda b, i, pt: ((b * nblk + i) // spq, 0, 0))

    kernel = functools.partial(_attn_kernel, pages_per_step=pps, steps_per_seq=spq, n_past_blocks=n_past_blocks)
    blocks_per_step = pps // pages_per_block
    return pl.pallas_call(
        kernel,
        out_shape=[jax.ShapeDtypeStruct(x.shape, F32), jax.ShapeDtypeStruct((db, nq, KV_WIDTH), F32)],
        grid_spec=pltpu.PrefetchScalarGridSpec(
            num_scalar_prefetch=1,
            grid=(batch, nblk),
            in_specs=([tile(N_HEADS * LANES),
                       per_batch(MOBA_BLOCK, KV_WIDTH),
                       per_batch(KV_WIDTH, MOBA_BLOCK),
                       pl.BlockSpec((None, nblk, KV_WIDTH), lambda b, i, pt: (b, 0, 0)),
                       tile(ATT_WIDTH), tile(GM_WIDTH), tile(d_model), full(w_out16), full(g_post),
                       per_seq(nq), per_seq(t_new), per_seq(t_new),
                       pl.BlockSpec(memory_space=pl.ANY), pl.BlockSpec(memory_space=pl.ANY)]),
            out_specs=[tile(d_model), per_seq(nq)],
            scratch_shapes=[pltpu.VMEM((N_HEADS, nblk, MOBA_BLOCK), F32),
                            pltpu.VMEM((N_HEADS, nblk, MOBA_BLOCK), F32),
                            pltpu.VMEM((N_HEADS, 1, MOBA_BLOCK), F32),
                            pltpu.VMEM((N_HEADS, HEAD_DIM + L_ROWS, MOBA_BLOCK), F32),
                            pltpu.VMEM((ATT_WIDTH, MOBA_BLOCK), F32),
                            pltpu.VMEM((N_HEADS, MOBA_BLOCK, MOBA_BLOCK), BF16),
                            pltpu.VMEM((N_HEADS, MOBA_BLOCK, MOBA_BLOCK), BF16),
                            pltpu.VMEM((N_HEADS, 1, MOBA_BLOCK), F32),
                            pltpu.VMEM((N_HEADS, 1, MOBA_BLOCK), F32),
                            pltpu.VMEM((n_past_blocks, nq, KV_WIDTH), F32),
                            pltpu.VMEM((nq, LANES), F32),
                            pltpu.VMEM((nq, LANES), F32),
                            pltpu.VMEM((nq, LANES), F32),
                            pltpu.VMEM((blocks_per_step, KV_WIDTH, MOBA_BLOCK), BF16),
                            pltpu.VMEM((blocks_per_step, KV_WIDTH, MOBA_BLOCK), BF16),
                            pltpu.VMEM((2, pps, KV_WIDTH, page_size), F32),
                            pltpu.VMEM((2, pps, KV_WIDTH, page_size), F32),
                            pltpu.SemaphoreType.DMA((2, 2))]),
        compiler_params=pltpu.CompilerParams(dimension_semantics=("arbitrary", "arbitrary"),
                                             vmem_limit_bytes=VMEM_LIMIT_BYTES),
        name="attn",
    )(page_table, qpad, kb.reshape(batch, nblk, MOBA_BLOCK, KV_WIDTH), vt.reshape(batch, nblk, KV_WIDTH, MOBA_BLOCK),
      kmean.reshape(batch, nblk, KV_WIDTH), sga, gmg, x, w_out16, g_post, qbd, k_new, v_new, cache_k, cache_v)


def _sample_step(qbd_ref, knew_ref, vnew_ref, k_pages, v_pages, out_ref, o_ref, m_ref, l_ref, g_ref,
                 kt16_ref, vt16_ref, *, s_idx, n_steps, n_past_blocks, page_size):
    pages_per_block = MOBA_BLOCK // page_size
    blocks_per_step = len(k_pages) // pages_per_block
    nq = qbd_ref.shape[0]
    t_new = knew_ref.shape[0]
    lane = lax.broadcasted_iota(jnp.int32, (nq, LANES), 1)

    qbd = qbd_ref[...]
    qbd16 = qbd.astype(BF16)
    for jj in range(blocks_per_step):
        for r in range(pages_per_block):
            cols = slice(r * page_size, (r + 1) * page_size)
            kt16_ref[jj, :, cols] = k_pages[jj * pages_per_block + r][...].astype(BF16)
            vt16_ref[jj, :, cols] = v_pages[jj * pages_per_block + r][...].astype(BF16)
    s_blk = [_dot(qbd16, kt16_ref[jj]) for jj in range(blocks_per_step)]
    m_blk = [jnp.max(s, axis=-1, keepdims=True) for s in s_blk]
    g_blk = [jnp.sum(s, axis=-1, keepdims=True) * np.float32(1.0 / MOBA_BLOCK) for s in s_blk]
    p_blk = [jnp.exp2(s - m) for s, m in zip(s_blk, m_blk)]
    l_blk = [jnp.sum(p, axis=-1, keepdims=True) for p in p_blk]
    fresh = s_idx == 0
    m_all = jnp.where(fresh, NEG_INF, m_ref[...])
    l_all = jnp.where(fresh, 0.0, l_ref[...])
    g_all = jnp.where(fresh, NEG_INF, g_ref[...])
    for jj in range(blocks_per_step):
        blk = s_idx * blocks_per_step + jj
        o_ref[blk] = _dot_nt(p_blk[jj].astype(BF16), vt16_ref[jj])
        hit = lane == blk
        m_all = jnp.where(hit, m_blk[jj], m_all)
        l_all = jnp.where(hit, l_blk[jj], l_all)
        g_all = jnp.where(hit, g_blk[jj], g_all)
    m_ref[...], l_ref[...], g_ref[...] = m_all, l_all, g_all

    @pl.when(s_idx == n_steps - 1)
    def _():
        g = g_ref[...]
        lane_f = lane.astype(F32)
        sel = jnp.zeros((nq, LANES), jnp.bool_)
        for _ in range(min(MOBA_TOP_K, n_past_blocks)):
            best = jnp.max(g, axis=-1, keepdims=True)
            idx = jnp.min(jnp.where(g == best, lane_f, np.float32(LANES)), axis=-1, keepdims=True)
            pick = lane_f == idx
            sel = sel | pick
            g = jnp.where(pick, NEG_INF, g)
        row = lax.broadcasted_iota(jnp.int32, (nq, LANES), 0)
        rows_per_token = nq // t_new
        s_own = jnp.full((nq, LANES), NEG_INF, F32)
        for t in range(t_new):
            s_t = jnp.sum(qbd * knew_ref[t:t + 1, :], axis=-1, keepdims=True)
            s_own = jnp.where((lane == t) & (row >= t * rows_per_token), s_t, s_own)
        m_blocks = m_ref[...]
        m_tot = jnp.maximum(jnp.max(jnp.where(sel, m_blocks, NEG_INF), axis=-1, keepdims=True),
                            jnp.max(s_own, axis=-1, keepdims=True))
        w = jnp.where(sel, jnp.exp2(m_blocks - m_tot), 0.0)
        p_own = jnp.exp2(s_own - m_tot)
        l_tot = (jnp.sum(w * l_ref[...], axis=-1, keepdims=True) + jnp.sum(p_own, axis=-1, keepdims=True))
        acc = jnp.zeros((nq, KV_WIDTH), F32)
        for n in range(n_past_blocks):
            acc = acc + jnp.sum(jnp.where(lane == n, w, 0.0), axis=-1, keepdims=True) * o_ref[n]
        for t in range(t_new):
            acc = acc + jnp.sum(jnp.where(lane == t, p_own, 0.0), axis=-1, keepdims=True) * vnew_ref[t:t + 1, :]
        out_ref[...] = acc / l_tot


def _sample_tail_kernel(att_ref, sga_ref, gmg_ref, x_ref, wout_ref, gpost_ref, y_ref):
    y_ref[...] = _tail(att_ref[...], sga_ref, gmg_ref, x_ref, wout_ref, gpost_ref)


def _sample_tail(att, sga, gmg, x, w_out16, g_post):
    return pl.pallas_call(
        _sample_tail_kernel,
        out_shape=jax.ShapeDtypeStruct(x.shape, F32),
        compiler_params=pltpu.CompilerParams(vmem_limit_bytes=VMEM_LIMIT_BYTES),
        name="sample_tail",
    )(att, sga, gmg, x, w_out16, g_post)


def _pair_cat(w):
    g, c, _ = w.shape
    return w.reshape(g // 2, 2, c, c).transpose(0, 2, 1, 3).reshape(g // 2, c, 2 * c)


def _layer(x_p, x_s, cache_k, cache_v, page_table, g_pre, w_in, g_sgu, w_sp, b_sp, w_out, g_post):
    batch, seq, d_model = x_p.shape
    db, t_new, _ = x_s.shape
    assert seq % MOBA_BLOCK == 0 and GM_CHUNK % t_new == 0
    w_in16 = w_in.astype(BF16)
    w_out16 = w_out.astype(BF16)
    g_pre2, g_sgu2, g_post2 = g_pre[None, :], g_sgu[None, :], g_post[None, :]
    grp = jnp.arange(MXU_COLS) // GM_GROUP_DIM
    gavg = jnp.where(grp[:, None] == grp[None, :], 1.0 / GM_GROUP_DIM, 0.0).astype(BF16)
    bias_p = jnp.repeat(b_sp.T, GM_GROUP_DIM, axis=1)
    wcat_p = _pair_cat(w_sp)
    seqs = GM_CHUNK // t_new
    w_s = jnp.einsum('ab,gts->gatbs', jnp.eye(seqs, dtype=F32), w_sp[:, :t_new, :t_new])
    wcat_s = _pair_cat(w_s.reshape(GM_GROUPS, GM_CHUNK, GM_CHUNK))
    bias_s = jnp.tile(bias_p[:t_new], (seqs, 1))

    xp2 = x_p.reshape(batch * seq, d_model)
    qpad, kt_p, vt_p, kb, vt, kmean, sga_p, gmg_p = _in_proj(
        xp2, g_pre2, w_in16, g_sgu2, gavg, wcat_p, bias_p, rows=PROMPT_ROWS, sample=False, seq=seq)

    xs2 = x_s.reshape(db * t_new, d_model)
    q_s, k_s, v_s, sga_s, gmg_s, vn_s = _in_proj(
        xs2, g_pre2, w_in16, g_sgu2, gavg, wcat_s, bias_s, rows=db * t_new, sample=True)
    q5 = q_s.reshape(db, t_new, N_KV_HEADS, KV_GROUP, HEAD_DIM)
    eye = jnp.eye(N_KV_HEADS, dtype=F32)[None, None, :, None, :, None]
    qbd = (q5[:, :, :, :, None, :] * eye).reshape(db, t_new * N_HEADS, KV_WIDTH)
    n_phys, page_size = cache_k.shape[:2]
    ck = cache_k.transpose(0, 2, 3, 1).reshape(n_phys, KV_WIDTH, page_size)
    cv = cache_v.transpose(0, 2, 3, 1).reshape(n_phys, KV_WIDTH, page_size)
    y_p, o_full = _attn(page_table, qpad, kb, vt, kmean, sga_p, gmg_p, xp2, w_out16, g_post2, qbd,
                        k_s.reshape(db, t_new, KV_WIDTH), v_s.reshape(db, t_new, KV_WIDTH), ck, cv,
                        batch=batch, seq=seq)
    o6 = o_full.reshape(db, t_new, N_KV_HEADS, KV_GROUP, N_KV_HEADS, HEAD_DIM)
    att_s = jnp.stack([o6[:, :, kh, :, kh, :] for kh in range(N_KV_HEADS)], axis=2)
    att_s = att_s.reshape(db * t_new, ATT_WIDTH)
    y_s = _sample_tail(att_s, sga_s, gmg_s, xs2, w_out16, g_post2)

    rows_of = lambda t: t.reshape(batch, N_KV_HEADS, HEAD_DIM, seq).transpose(0, 3, 1, 2)
    kv_s = (db, t_new, N_KV_HEADS, HEAD_DIM)
    return (y_p.reshape(batch, seq, d_model), y_s.reshape(db, t_new, d_model),
            rows_of(kt_p), rows_of(vt_p), k_s.reshape(kv_s), v_s.reshape(kv_s),
            vn_s.reshape(db, t_new, GM_WIDTH))


def kernel(x_prompt, x_sample, cache_k, cache_v, page_table, g_pre, w_in, g_sgu, w_spatial, b_spatial, w_out, g_post):
    yp, ys = x_prompt, x_sample
    outs = []
    for l in range(w_in.shape[0]):
        yp, ys, kp, vp, kn, vn, sv = _layer(yp, ys, cache_k[l], cache_v[l], page_table, g_pre[l], w_in[l],
                                            g_sgu[l], w_spatial[l], b_spatial[l], w_out[l], g_post[l])
        outs.append((kp, vp, kn, vn, sv))
    stacked = [jnp.stack(parts) for parts in zip(*outs)]
    return (yp, ys, *stacked)
```

```python
import functools

import numpy as np
import jax
import jax.numpy as jnp
from jax import lax
from jax.experimental import pallas as pl
from jax.experimental.pallas import tpu as pltpu

N_HEADS = 8
HEAD_DIM = 64
N_KV_HEADS = 4
KV_GROUP = N_HEADS // N_KV_HEADS
ATT_WIDTH = N_HEADS * HEAD_DIM
KV_WIDTH = N_KV_HEADS * HEAD_DIM
MOBA_BLOCK = 256
MOBA_TOP_K = 3
GM_GROUPS = 8
GM_GROUP_DIM = 64
GM_WIDTH = GM_GROUPS * GM_GROUP_DIM
GM_CHUNK = 128
NORM_EPS = 1e-6
LN_EPS = 1e-5

LANES = 128
MXU_COLS = 256
VMEM_LIMIT_BYTES = 56 * 1024 * 1024
PROMPT_ROWS = 1024
L_ROWS = 16

F32 = jnp.float32
BF16 = jnp.bfloat16
NEG_INF = float("-inf")
Q_SCALE = np.float32(HEAD_DIM ** -0.5 * np.log2(np.e))


def _dot(a, b):
    return jnp.dot(a, b, preferred_element_type=F32)


def _dot_nt(a, b):
    return lax.dot_general(a, b, (((1,), (1,)), ((), ())), preferred_element_type=F32)


def _gelu_tanh(x):
    c = np.float32(np.sqrt(2.0 / np.pi))
    return x * (0.5 * (1.0 + jnp.tanh(c * (x + np.float32(0.044715) * (x * x * x)))))


def _silu(x):
    return x / (1.0 + jnp.exp(-x))


def _rms_norm(x, g):
    ms = jnp.mean(x * x, axis=-1, keepdims=True)
    return x * lax.rsqrt(ms + NORM_EPS) * g


def _in_proj_kernel(x_ref, gpre_ref, win_ref, gsgu_ref, gavg_ref, wcat_ref, bias_ref, *out_refs, sample):
    if sample:
        q_ref, k_ref, v_ref, sga_ref, gmg_ref, vn_ref = out_refs
    else:
        q_ref, k_ref, v_ref, kb_ref, vt_ref, kmean_ref, sga_ref, gmg_ref = out_refs
    rows = x_ref.shape[0]
    h = _rms_norm(x_ref[...], gpre_ref[...]).astype(BF16)

    def proj(lo, width):
        return _dot(h, win_ref[:, lo:lo + width])

    o_k = ATT_WIDTH
    o_v = o_k + KV_WIDTH
    o_ga = o_v + KV_WIDTH
    o_u = o_ga + ATT_WIDTH
    o_vs = o_u + GM_WIDTH
    o_gg = o_vs + GM_WIDTH

    zq = proj(0, ATT_WIDTH) * Q_SCALE
    if sample:
        q_ref[...] = zq
    else:
        first_half = lax.broadcasted_iota(jnp.int32, (rows, LANES), 1) < HEAD_DIM
        for kh in range(N_KV_HEADS):
            tile = zq[:, kh * LANES:(kh + 1) * LANES]
            swapped = pltpu.roll(tile, HEAD_DIM, axis=1)
            keep = first_half if kh % 2 == 0 else jnp.logical_not(first_half)
            even_src, odd_src = (tile, swapped) if kh % 2 == 0 else (swapped, tile)
            h0 = 2 * kh
            q_ref[:, h0 * LANES:(h0 + 1) * LANES] = jnp.where(keep, even_src, 0.0).astype(BF16)
            q_ref[:, (h0 + 1) * LANES:(h0 + 2) * LANES] = jnp.where(keep, odd_src, 0.0).astype(BF16)

    zk = proj(o_k, KV_WIDTH)
    zv = proj(o_v, KV_WIDTH)
    if sample:
        k_ref[...] = zk
        v_ref[...] = zv
    else:
        k_ref[...] = zk.T
        zvt = zv.T
        v_ref[...] = zvt
        kb_ref[...] = zk.astype(BF16)
        for c in range(rows // MOBA_BLOCK):
            blk = slice(c * MOBA_BLOCK, (c + 1) * MOBA_BLOCK)
            kmean_ref[c] = jnp.sum(zk[blk], axis=0, keepdims=True) * np.float32(1.0 / MOBA_BLOCK)
            vt_ref[c] = zvt[:, blk].astype(BF16)

    sga_ref[...] = _silu(proj(o_ga, ATT_WIDTH)).astype(BF16)

    u = _gelu_tanh(proj(o_u, GM_WIDTH))
    gv = _gelu_tanh(proj(o_vs, GM_WIDTH))
    def group_mean(t):
        t16 = t.astype(BF16)
        return jnp.concatenate([_dot(t16[:, c:c + MXU_COLS], gavg_ref[...])
                                for c in range(0, GM_WIDTH, MXU_COLS)], axis=1)

    mu = group_mean(gv)
    d = gv - mu
    var = group_mean(d * d)
    vn = d * lax.rsqrt(var + LN_EPS) * gsgu_ref[...]
    if sample:
        vn_ref[...] = vn

    r_i = lax.broadcasted_iota(jnp.int32, (GM_CHUNK, 2 * GM_CHUNK), 0)
    c_i = lax.broadcasted_iota(jnp.int32, (GM_CHUNK, 2 * GM_CHUNK), 1)
    tril = r_i >= (c_i & (GM_CHUNK - 1))
    lane_lo = lax.broadcasted_iota(jnp.int32, (GM_CHUNK, LANES), 1) < GM_GROUP_DIM
    sgg = _silu(proj(o_gg, GM_WIDTH))
    n_chunks = rows // GM_CHUNK
    chunks_per_dot = 2 if n_chunks % 2 == 0 else 1
    for p in range(GM_GROUPS // 2):
        w_pair = jnp.where(tril, wcat_ref[p], 0.0).astype(BF16)
        cols = slice(p * LANES, (p + 1) * LANES)

        def split_groups(c):
            vp = vn[c * GM_CHUNK:(c + 1) * GM_CHUNK, cols]
            return jnp.concatenate([jnp.where(lane_lo, vp, 0.0).astype(BF16),
                                    jnp.where(lane_lo, 0.0, vp).astype(BF16)], axis=0)

        for c0 in range(0, n_chunks, chunks_per_dot):
            cs = range(c0, c0 + chunks_per_dot)
            mixed_all = _dot(w_pair, jnp.concatenate([split_groups(c) for c in cs], axis=1))
            for k, c in enumerate(cs):
                rws = slice(c * GM_CHUNK, (c + 1) * GM_CHUNK)
                mixed = mixed_all[:, k * LANES:(k + 1) * LANES] + bias_ref[:, cols]
                gmg_ref[rws, cols] = (u[rws, cols] * mixed * sgg[rws, cols]).astype(BF16)


def _in_proj(x, g_pre, w_in16, g_sgu, gavg, wcat, bias, *, rows, sample, seq=None):
    n, d_model = x.shape
    assert n % rows == 0 and rows % GM_CHUNK == 0
    steps = n // rows
    row_spec = lambda width: pl.BlockSpec((rows, width), lambda i: (i, 0))
    full = lambda a: pl.BlockSpec(a.shape, lambda i: (0,) * a.ndim)
    if sample:
        out_shape = [jax.ShapeDtypeStruct((n, ATT_WIDTH), F32),
                     jax.ShapeDtypeStruct((n, KV_WIDTH), F32),
                     jax.ShapeDtypeStruct((n, KV_WIDTH), F32),
                     jax.ShapeDtypeStruct((n, ATT_WIDTH), BF16),
                     jax.ShapeDtypeStruct((n, GM_WIDTH), BF16),
                     jax.ShapeDtypeStruct((n, GM_WIDTH), F32)]
        out_specs = [row_spec(ATT_WIDTH), row_spec(KV_WIDTH), row_spec(KV_WIDTH),
                     row_spec(ATT_WIDTH), row_spec(GM_WIDTH), row_spec(GM_WIDTH)]
    else:
        assert rows % MOBA_BLOCK == 0 and seq % rows == 0
        bps = rows // MOBA_BLOCK
        nblk = n // MOBA_BLOCK
        spb = seq // rows
        kv_t_spec = pl.BlockSpec((None, KV_WIDTH, rows), lambda i: (i // spb, 0, i % spb))
        out_shape = [jax.ShapeDtypeStruct((n, N_HEADS * LANES), BF16),
                     jax.ShapeDtypeStruct((n // seq, KV_WIDTH, seq), F32),
                     jax.ShapeDtypeStruct((n // seq, KV_WIDTH, seq), F32),
                     jax.ShapeDtypeStruct((n, KV_WIDTH), BF16),
                     jax.ShapeDtypeStruct((nblk, KV_WIDTH, MOBA_BLOCK), BF16),
                     jax.ShapeDtypeStruct((nblk, 1, KV_WIDTH), F32),
                     jax.ShapeDtypeStruct((n, ATT_WIDTH), BF16),
                     jax.ShapeDtypeStruct((n, GM_WIDTH), BF16)]
        out_specs = [row_spec(N_HEADS * LANES), kv_t_spec, kv_t_spec, row_spec(KV_WIDTH),
                     pl.BlockSpec((bps, KV_WIDTH, MOBA_BLOCK), lambda i: (i, 0, 0)),
                     pl.BlockSpec((bps, 1, KV_WIDTH), lambda i: (i, 0, 0)),
                     row_spec(ATT_WIDTH), row_spec(GM_WIDTH)]
    return pl.pallas_call(
        functools.partial(_in_proj_kernel, sample=sample),
        out_shape=out_shape,
        grid=(steps,),
        in_specs=[row_spec(d_model), full(g_pre), full(w_in16), full(g_sgu), full(gavg), full(wcat), full(bias)],
        out_specs=out_specs,
        compiler_params=pltpu.CompilerParams(dimension_semantics=("arbitrary",),
                                             vmem_limit_bytes=VMEM_LIMIT_BYTES),
        name="in_proj_sample" if sample else "in_proj_prompt",
    )(x, g_pre, w_in16, g_sgu, gavg, wcat, bias)


def _tail(att, sga_ref, gmg_ref, x_ref, wout_ref, gpost_ref):
    mix = jnp.concatenate([(att * sga_ref[...].astype(F32)).astype(BF16), gmg_ref[...]], axis=-1)
    out = _dot(mix, wout_ref[...])
    return x_ref[...] + _rms_norm(out, gpost_ref[...])


def _attn_kernel(pt_ref, q_ref, kb_ref, vt_ref, kmean_ref, sga_ref, gmg_ref, x_ref, wout_ref, gpost_ref,
                 qbd_ref, knew_ref, vnew_ref, ck_hbm, cv_hbm,
                 y_ref, sout_ref, sel_ref, g_ref, m_ref, acc_ref, attt_ref, sa_ref, sb_ref, bma_ref, bmb_ref,
                 so_ref, sm_ref, sl_ref, sg_ref, kt16_ref, vt16_ref, kbuf_ref, vbuf_ref, page_sem,
                 *, pages_per_step, steps_per_seq, n_past_blocks):
    i = pl.program_id(1)
    page_size = kbuf_ref.shape[-1]

    step = pl.program_id(0) * pl.num_programs(1) + i
    n_steps = pl.num_programs(0) * pl.num_programs(1)
    slot = step & 1

    def page_copies(for_step, into_slot):
        seq = for_step // steps_per_seq
        first = lax.rem(for_step, jnp.int32(steps_per_seq)) * pages_per_step
        copies = []
        for r in range(pages_per_step):
            page = pt_ref[seq, first + r]
            copies.append(pltpu.make_async_copy(ck_hbm.at[page], kbuf_ref.at[into_slot, r], page_sem.at[into_slot, 0]))
            copies.append(pltpu.make_async_copy(cv_hbm.at[page], vbuf_ref.at[into_slot, r], page_sem.at[into_slot, 1]))
        return copies

    @pl.when(step == 0)
    def _():
        for c in page_copies(step, slot):
            c.start()

    for c in page_copies(step, slot):
        c.wait()
    next_step = jnp.where(step + 1 < n_steps, step + 1, 0)
    _sample_step(qbd_ref, knew_ref, vnew_ref,
                 [kbuf_ref.at[slot, r] for r in range(pages_per_step)],
                 [vbuf_ref.at[slot, r] for r in range(pages_per_step)],
                 sout_ref, so_ref, sm_ref, sl_ref, sg_ref, kt16_ref, vt16_ref,
                 s_idx=lax.rem(step, jnp.int32(steps_per_seq)), n_steps=steps_per_seq,
                 n_past_blocks=n_past_blocks, page_size=page_size)

    nblk = kb_ref.shape[0]
    tq = q_ref.shape[0]
    ones_rows = jnp.ones((L_ROWS, MOBA_BLOCK), BF16)

    def q_head(h):
        return q_ref[:, h * LANES:(h + 1) * LANES]

    def k_tile(j, h):
        a = (h // KV_GROUP) // 2
        return kb_ref[j, :, a * LANES:(a + 1) * LANES]

    def vt_aug(j, h):
        kh = h // KV_GROUP
        return jnp.concatenate([vt_ref[j, kh * HEAD_DIM:(kh + 1) * HEAD_DIM, :], ones_rows], axis=0)

    blk_i = lax.broadcasted_iota(jnp.int32, (nblk, tq), 0)
    km = kmean_ref[...]

    causal = (lax.broadcasted_iota(jnp.int32, (MOBA_BLOCK, tq), 0)
              <= lax.broadcasted_iota(jnp.int32, (MOBA_BLOCK, tq), 1))

    head_pairs = [(KV_GROUP * kh, KV_GROUP * kh + 1) for kh in range(N_KV_HEADS)]
    bufs = ((sa_ref, bma_ref), (sb_ref, bmb_ref))

    def scores(j, mask, buf, heads):
        s_buf, bm_buf = buf
        for h in heads:
            s = _dot_nt(k_tile(j, h), q_head(h))
            if mask is not None:
                s = jnp.where(mask, s, NEG_INF)
            s16 = s.astype(BF16)
            s_buf[h] = s16
            bm_buf[h] = jnp.max(s16, axis=0, keepdims=True).astype(F32)

    def attend_own(buf, heads):
        s_buf, bm_buf = buf
        for h in heads:
            m = bm_buf[h]
            m_ref[h] = m
            acc_ref[h] = _dot(vt_aug(i, h), jnp.exp2(s_buf[h] - m.astype(BF16)))

    def attend(j, buf, heads):
        s_buf, bm_buf = buf
        for h in heads:
            chosen = sel_ref[h, pl.ds(j, 1), :] > 0.5
            m_old = m_ref[h]
            m_new = jnp.where(chosen, jnp.maximum(m_old, bm_buf[h]), m_old)
            p = jnp.exp2(s_buf[h] - jnp.where(chosen, m_new, jnp.inf).astype(BF16))
            acc_ref[h] = acc_ref[h] * jnp.exp2(m_old - m_new) + _dot(vt_aug(j, h), p)
            m_ref[h] = m_new

    def overlapped(attend_pair, j_next, buf_next):
        ahead = 2
        for pair in head_pairs[:ahead]:
            scores(j_next, None, buf_next, pair)
        for k, pair in enumerate(head_pairs):
            attend_pair(pair)
            if k + ahead < len(head_pairs):
                scores(j_next, None, buf_next, head_pairs[k + ahead])

    for c in page_copies(next_step, 1 - slot):
        c.start()

    for h in range(N_HEADS):
        a = (h // KV_GROUP) // 2
        km_a = km[:, a * LANES:(a + 1) * LANES]
        km_hi = km_a.astype(BF16)
        km_lo = (km_a - km_hi.astype(F32)).astype(BF16)
        s_aug = _dot_nt(jnp.concatenate([k_tile(i, h), km_hi, km_lo], axis=0), q_head(h))
        g_ref[h] = s_aug[MOBA_BLOCK:MOBA_BLOCK + nblk] + s_aug[MOBA_BLOCK + nblk:]
        s16 = jnp.where(causal, s_aug[:MOBA_BLOCK], NEG_INF).astype(BF16)
        sa_ref[h] = s16
        bma_ref[h] = jnp.max(s16, axis=0, keepdims=True).astype(F32)

    def count_beaten(n2, ranks):
        tie_wins = blk_i > n2
        out = []
        for h in range(N_HEADS):
            g = g_ref[h]
            gb = g_ref[h, pl.ds(n2, 1), :]
            out.append(ranks[h] + jnp.where(tie_wins, jnp.where(g > gb, 0.0, 1.0), jnp.where(gb > g, 1.0, 0.0)))
        return tuple(out)

    ranks = lax.fori_loop(0, i, count_beaten, tuple(jnp.zeros((nblk, tq), F32) for _ in range(N_HEADS)))
    for h in range(N_HEADS):
        sel_ref[h] = jnp.where((blk_i < i) & (ranks[h] < MOBA_TOP_K), 1.0, 0.0)

    overlapped(lambda pair: attend_own(bufs[0], pair), 0, bufs[1])

    def past_pair(t, carry):
        j0 = 2 * t
        overlapped(lambda pair: attend(j0, bufs[1], pair), j0 + 1, bufs[0])
        overlapped(lambda pair: attend(j0 + 1, bufs[0], pair), j0 + 2, bufs[1])
        return carry

    lax.fori_loop(0, lax.shift_right_logical(i, jnp.int32(1)), past_pair, 0)

    @pl.when((i & 1) == 1)
    def _():
        attend(i - 1, bufs[1], range(N_HEADS))

    for h in range(N_HEADS):
        acc = acc_ref[h]
        attt_ref[h * HEAD_DIM:(h + 1) * HEAD_DIM, :] = acc[:HEAD_DIM] / acc[HEAD_DIM:HEAD_DIM + 1]
    y_ref[...] = _tail(attt_ref[...].T, sga_ref, gmg_ref, x_ref, wout_ref, gpost_ref)

    @pl.when(step == n_steps - 1)
    def _():
        for c in page_copies(next_step, 1 - slot):
            c.wait()


def _attn(page_table, qpad, kb, vt, kmean, sga, gmg, x, w_out16, g_post, qbd, k_new, v_new, cache_k, cache_v,
          *, batch, seq):
    nblk = seq // MOBA_BLOCK
    d_model = x.shape[-1]
    steps = batch * nblk
    db, n_pages = page_table.shape
    page_size = cache_k.shape[2]
    nq = qbd.shape[1]
    t_new = k_new.shape[1]
    pages_per_block = MOBA_BLOCK // page_size
    n_past_blocks = n_pages // pages_per_block
    assert (db * n_pages) % steps == 0
    pps = db * n_pages // steps
    assert n_pages % pps == 0 and pps % pages_per_block == 0
    spq = n_pages // pps
    assert n_past_blocks <= LANES and t_new <= LANES

    tile = lambda width: pl.BlockSpec((MOBA_BLOCK, width), lambda b, i, pt: (b * nblk + i, 0))
    per_batch = lambda r, c: pl.BlockSpec((None, nblk, r, c), lambda b, i, pt: (b, 0, 0, 0))
    full = lambda a: pl.BlockSpec(a.shape, lambda b, i, pt: (0,) * a.ndim)
    per_seq = lambda rows: pl.BlockSpec((None, rows, KV_WIDTH), lambda b, i, pt: ((b * nblk + i) // spq, 0, 0))

    kernel = functools.partial(_attn_kernel, pages_per_step=pps, steps_per_seq=spq, n_past_blocks=n_past_blocks)
    blocks_per_step = pps // pages_per_block
    return pl.pallas_call(
        kernel,
        out_shape=[jax.ShapeDtypeStruct(x.shape, F32), jax.ShapeDtypeStruct((db, nq, KV_WIDTH), F32)],
        grid_spec=pltpu.PrefetchScalarGridSpec(
            num_scalar_prefetch=1,
            grid=(batch, nblk),
            in_specs=([tile(N_HEADS * LANES),
                       per_batch(MOBA_BLOCK, KV_WIDTH),
                       per_batch(KV_WIDTH, MOBA_BLOCK),
                       pl.BlockSpec((None, nblk, KV_WIDTH), lambda b, i, pt: (b, 0, 0)),
                       tile(ATT_WIDTH), tile(GM_WIDTH), tile(d_model), full(w_out16), full(g_post),
                       per_seq(nq), per_seq(t_new), per_seq(t_new),
                       pl.BlockSpec(memory_space=pl.ANY), pl.BlockSpec(memory_space=pl.ANY)]),
            out_specs=[tile(d_model), per_seq(nq)],
            scratch_shapes=[pltpu.VMEM((N_HEADS, nblk, MOBA_BLOCK), F32),
                            pltpu.VMEM((N_HEADS, nblk, MOBA_BLOCK), F32),
                            pltpu.VMEM((N_HEADS, 1, MOBA_BLOCK), F32),
                            pltpu.VMEM((N_HEADS, HEAD_DIM + L_ROWS, MOBA_BLOCK), F32),
                            pltpu.VMEM((ATT_WIDTH, MOBA_BLOCK), F32),
                            pltpu.VMEM((N_HEADS, MOBA_BLOCK, MOBA_BLOCK), BF16),
                            pltpu.VMEM((N_HEADS, MOBA_BLOCK, MOBA_BLOCK), BF16),
                            pltpu.VMEM((N_HEADS, 1, MOBA_BLOCK), F32),
                            pltpu.VMEM((N_HEADS, 1, MOBA_BLOCK), F32),
                            pltpu.VMEM((n_past_blocks, nq, KV_WIDTH), F32),
                            pltpu.VMEM((nq, LANES), F32),
                            pltpu.VMEM((nq, LANES), F32),
                            pltpu.VMEM((nq, LANES), F32),
                            pltpu.VMEM((blocks_per_step, KV_WIDTH, MOBA_BLOCK), BF16),
                            pltpu.VMEM((blocks_per_step, KV_WIDTH, MOBA_BLOCK), BF16),
                            pltpu.VMEM((2, pps, KV_WIDTH, page_size), F32),
                            pltpu.VMEM((2, pps, KV_WIDTH, page_size), F32),
                            pltpu.SemaphoreType.DMA((2, 2))]),
        compiler_params=pltpu.CompilerParams(dimension_semantics=("arbitrary", "arbitrary"),
                                             vmem_limit_bytes=VMEM_LIMIT_BYTES),
        name="attn",
    )(page_table, qpad, kb.reshape(batch, nblk, MOBA_BLOCK, KV_WIDTH), vt.reshape(batch, nblk, KV_WIDTH, MOBA_BLOCK),
      kmean.reshape(batch, nblk, KV_WIDTH), sga, gmg, x, w_out16, g_post, qbd, k_new, v_new, cache_k, cache_v)


def _sample_step(qbd_ref, knew_ref, vnew_ref, k_pages, v_pages, out_ref, o_ref, m_ref, l_ref, g_ref,
                 kt16_ref, vt16_ref, *, s_idx, n_steps, n_past_blocks, page_size):
    pages_per_block = MOBA_BLOCK // page_size
    blocks_per_step = len(k_pages) // pages_per_block
    nq = qbd_ref.shape[0]
    t_new = knew_ref.shape[0]
    lane = lax.broadcasted_iota(jnp.int32, (nq, LANES), 1)

    qbd = qbd_ref[...]
    qbd16 = qbd.astype(BF16)
    for jj in range(blocks_per_step):
        for r in range(pages_per_block):
            cols = slice(r * page_size, (r + 1) * page_size)
            kt16_ref[jj, :, cols] = k_pages[jj * pages_per_block + r][...].astype(BF16)
            vt16_ref[jj, :, cols] = v_pages[jj * pages_per_block + r][...].astype(BF16)
    s_blk = [_dot(qbd16, kt16_ref[jj]) for jj in range(blocks_per_step)]
    m_blk = [jnp.max(s, axis=-1, keepdims=True) for s in s_blk]
    g_blk = [jnp.sum(s, axis=-1, keepdims=True) * np.float32(1.0 / MOBA_BLOCK) for s in s_blk]
    p_blk = [jnp.exp2(s - m) for s, m in zip(s_blk, m_blk)]
    l_blk = [jnp.sum(p, axis=-1, keepdims=True) for p in p_blk]
    fresh = s_idx == 0
    m_all = jnp.where(fresh, NEG_INF, m_ref[...])
    l_all = jnp.where(fresh, 0.0, l_ref[...])
    g_all = jnp.where(fresh, NEG_INF, g_ref[...])
    for jj in range(blocks_per_step):
        blk = s_idx * blocks_per_step + jj
        o_ref[blk] = _dot_nt(p_blk[jj].astype(BF16), vt16_ref[jj])
        hit = lane == blk
        m_all = jnp.where(hit, m_blk[jj], m_all)
        l_all = jnp.where(hit, l_blk[jj], l_all)
        g_all = jnp.where(hit, g_blk[jj], g_all)
    m_ref[...], l_ref[...], g_ref[...] = m_all, l_all, g_all

    @pl.when(s_idx == n_steps - 1)
    def _():
        g = g_ref[...]
        lane_f = lane.astype(F32)
        sel = jnp.zeros((nq, LANES), jnp.bool_)
        for _ in range(min(MOBA_TOP_K, n_past_blocks)):
            best = jnp.max(g, axis=-1, keepdims=True)
            idx = jnp.min(jnp.where(g == best, lane_f, np.float32(LANES)), axis=-1, keepdims=True)
            pick = lane_f == idx
            sel = sel | pick
            g = jnp.where(pick, NEG_INF, g)
        row = lax.broadcasted_iota(jnp.int32, (nq, LANES), 0)
        rows_per_token = nq // t_new
        s_own = jnp.full((nq, LANES), NEG_INF, F32)
        for t in range(t_new):
            s_t = jnp.sum(qbd * knew_ref[t:t + 1, :], axis=-1, keepdims=True)
            s_own = jnp.where((lane == t) & (row >= t * rows_per_token), s_t, s_own)
        m_blocks = m_ref[...]
        m_tot = jnp.maximum(jnp.max(jnp.where(sel, m_blocks, NEG_INF), axis=-1, keepdims=True),
                            jnp.max(s_own, axis=-1, keepdims=True))
        w = jnp.where(sel, jnp.exp2(m_blocks - m_tot), 0.0)
        p_own = jnp.exp2(s_own - m_tot)
        l_tot = (jnp.sum(w * l_ref[...], axis=-1, keepdims=True) + jnp.sum(p_own, axis=-1, keepdims=True))
        acc = jnp.zeros((nq, KV_WIDTH), F32)
        for n in range(n_past_blocks):
            acc = acc + jnp.sum(jnp.where(lane == n, w, 0.0), axis=-1, keepdims=True) * o_ref[n]
        for t in range(t_new):
            acc = acc + jnp.sum(jnp.where(lane == t, p_own, 0.0), axis=-1, keepdims=True) * vnew_ref[t:t + 1, :]
        out_ref[...] = acc / l_tot


def _sample_tail_kernel(att_ref, sga_ref, gmg_ref, x_ref, wout_ref, gpost_ref, y_ref):
    y_ref[...] = _tail(att_ref[...], sga_ref, gmg_ref, x_ref, wout_ref, gpost_ref)


def _sample_tail(att, sga, gmg, x, w_out16, g_post):
    return pl.pallas_call(
        _sample_tail_kernel,
        out_shape=jax.ShapeDtypeStruct(x.shape, F32),
        compiler_params=pltpu.CompilerParams(vmem_limit_bytes=VMEM_LIMIT_BYTES),
        name="sample_tail",
    )(att, sga, gmg, x, w_out16, g_post)


def _pair_cat(w):
    g, c, _ = w.shape
    return w.reshape(g // 2, 2, c, c).transpose(0, 2, 1, 3).reshape(g // 2, c, 2 * c)


def _layer(x_p, x_s, cache_k, cache_v, page_table, g_pre, w_in, g_sgu, w_sp, b_sp, w_out, g_post):
    batch, seq, d_model = x_p.shape
    db, t_new, _ = x_s.shape
    assert seq % MOBA_BLOCK == 0 and GM_CHUNK % t_new == 0
    w_in16 = w_in.astype(BF16)
    w_out16 = w_out.astype(BF16)
    g_pre2, g_sgu2, g_post2 = g_pre[None, :], g_sgu[None, :], g_post[None, :]
    grp = jnp.arange(MXU_COLS) // GM_GROUP_DIM
    gavg = jnp.where(grp[:, None] == grp[None, :], 1.0 / GM_GROUP_DIM, 0.0).astype(BF16)
    bias_p = jnp.repeat(b_sp.T, GM_GROUP_DIM, axis=1)
    wcat_p = _pair_cat(w_sp)
    seqs = GM_CHUNK // t_new
    w_s = jnp.einsum('ab,gts->gatbs', jnp.eye(seqs, dtype=F32), w_sp[:, :t_new, :t_new])
    wcat_s = _pair_cat(w_s.reshape(GM_GROUPS, GM_CHUNK, GM_CHUNK))
    bias_s = jnp.tile(bias_p[:t_new], (seqs, 1))

    xp2 = x_p.reshape(batch * seq, d_model)
    qpad, kt_p, vt_p, kb, vt, kmean, sga_p, gmg_p = _in_proj(
        xp2, g_pre2, w_in16, g_sgu2, gavg, wcat_p, bias_p, rows=PROMPT_ROWS, sample=False, seq=seq)

    xs2 = x_s.reshape(db * t_new, d_model)
    q_s, k_s, v_s, sga_s, gmg_s, vn_s = _in_proj(
        xs2, g_pre2, w_in16, g_sgu2, gavg, wcat_s, bias_s, rows=db * t_new, sample=True)
    q5 = q_s.reshape(db, t_new, N_KV_HEADS, KV_GROUP, HEAD_DIM)
    eye = jnp.eye(N_KV_HEADS, dtype=F32)[None, None, :, None, :, None]
    qbd = (q5[:, :, :, :, None, :] * eye).reshape(db, t_new * N_HEADS, KV_WIDTH)
    n_phys, page_size = cache_k.shape[:2]
    ck = cache_k.transpose(0, 2, 3, 1).reshape(n_phys, KV_WIDTH, page_size)
    cv = cache_v.transpose(0, 2, 3, 1).reshape(n_phys, KV_WIDTH, page_size)
    y_p, o_full = _attn(page_table, qpad, kb, vt, kmean, sga_p, gmg_p, xp2, w_out16, g_post2, qbd,
                        k_s.reshape(db, t_new, KV_WIDTH), v_s.reshape(db, t_new, KV_WIDTH), ck, cv,
                        batch=batch, seq=seq)
    o6 = o_full.reshape(db, t_new, N_KV_HEADS, KV_GROUP, N_KV_HEADS, HEAD_DIM)
    att_s = jnp.stack([o6[:, :, kh, :, kh, :] for kh in range(N_KV_HEADS)], axis=2)
    att_s = att_s.reshape(db * t_new, ATT_WIDTH)
    y_s = _sample_tail(att_s, sga_s, gmg_s, xs2, w_out16, g_post2)

    rows_of = lambda t: t.reshape(batch, N_KV_HEADS, HEAD_DIM, seq).transpose(0, 3, 1, 2)
    kv_s = (db, t_new, N_KV_HEADS, HEAD_DIM)
    return (y_p.reshape(batch, seq, d_model), y_s.reshape(db, t_new, d_model),
            rows_of(kt_p), rows_of(vt_p), k_s.reshape(kv_s), v_s.reshape(kv_s),
            vn_s.reshape(db, t_new, GM_WIDTH))


def kernel(x_prompt, x_sample, cache_k, cache_v, page_table, g_pre, w_in, g_sgu, w_spatial, b_spatial, w_out, g_post):
    yp, ys = x_prompt, x_sample
    outs = []
    for l in range(w_in.shape[0]):
        yp, ys, kp, vp, kn, vn, sv = _layer(yp, ys, cache_k[l], cache_v[l], page_table, g_pre[l], w_in[l],
                                            g_sgu[l], w_spatial[l], b_spatial[l], w_out[l], g_post[l])
        outs.append((kp, vp, kn, vn, sv))
    stacked = [jnp.stack(parts) for parts in zip(*outs)]
    return (yp, ys, *stacked)
```

```python
import functools

import numpy as np
import jax
import jax.numpy as jnp
from jax import lax
from jax.experimental import pallas as pl
from jax.experimental.pallas import tpu as pltpu

N_HEADS = 8
HEAD_DIM = 64
N_KV_HEADS = 4
KV_GROUP = N_HEADS // N_KV_HEADS
ATT_WIDTH = N_HEADS * HEAD_DIM
KV_WIDTH = N_KV_HEADS * HEAD_DIM
MOBA_BLOCK = 256
MOBA_TOP_K = 3
GM_GROUPS = 8
GM_GROUP_DIM = 64
GM_WIDTH = GM_GROUPS * GM_GROUP_DIM
GM_CHUNK = 128
NORM_EPS = 1e-6
LN_EPS = 1e-5

LANES = 128
MXU_COLS = 256
VMEM_LIMIT_BYTES = 56 * 1024 * 1024
PROMPT_ROWS = 1024
L_ROWS = 16

F32 = jnp.float32
BF16 = jnp.bfloat16
NEG_INF = float("-inf")
Q_SCALE = np.float32(HEAD_DIM ** -0.5 * np.log2(np.e))


def _dot(a, b):
    return jnp.dot(a, b, preferred_element_type=F32)


def _dot_nt(a, b):
    return lax.dot_general(a, b, (((1,), (1,)), ((), ())), preferred_element_type=F32)


def _gelu_tanh(x):
    c = np.float32(np.sqrt(2.0 / np.pi))
    return x * (0.5 * (1.0 + jnp.tanh(c * (x + np.float32(0.044715) * (x * x * x)))))


def _silu(x):
    return x / (1.0 + jnp.exp(-x))


def _rms_norm(x, g):
    ms = jnp.mean(x * x, axis=-1, keepdims=True)
    return x * lax.rsqrt(ms + NORM_EPS) * g


def _in_proj_kernel(x_ref, gpre_ref, win_ref, gsgu_ref, gavg_ref, wcat_ref, bias_ref, *out_refs, sample):
    if sample:
        q_ref, k_ref, v_ref, sga_ref, gmg_ref, vn_ref = out_refs
    else:
        q_ref, k_ref, v_ref, kb_ref, vt_ref, kmean_ref, sga_ref, gmg_ref = out_refs
    rows = x_ref.shape[0]
    h = _rms_norm(x_ref[...], gpre_ref[...]).astype(BF16)

    def proj(lo, width):
        return _dot(h, win_ref[:, lo:lo + width])

    o_k = ATT_WIDTH
    o_v = o_k + KV_WIDTH
    o_ga = o_v + KV_WIDTH
    o_u = o_ga + ATT_WIDTH
    o_vs = o_u + GM_WIDTH
    o_gg = o_vs + GM_WIDTH

    zq = proj(0, ATT_WIDTH) * Q_SCALE
    if sample:
        q_ref[...] = zq
    else:
        first_half = lax.broadcasted_iota(jnp.int32, (rows, LANES), 1) < HEAD_DIM
        for kh in range(N_KV_HEADS):
            tile = zq[:, kh * LANES:(kh + 1) * LANES]
            swapped = pltpu.roll(tile, HEAD_DIM, axis=1)
            keep = first_half if kh % 2 == 0 else jnp.logical_not(first_half)
            even_src, odd_src = (tile, swapped) if kh % 2 == 0 else (swapped, tile)
            h0 = 2 * kh
            q_ref[:, h0 * LANES:(h0 + 1) * LANES] = jnp.where(keep, even_src, 0.0).astype(BF16)
            q_ref[:, (h0 + 1) * LANES:(h0 + 2) * LANES] = jnp.where(keep, odd_src, 0.0).astype(BF16)

    zk = proj(o_k, KV_WIDTH)
    zv = proj(o_v, KV_WIDTH)
    if sample:
        k_ref[...] = zk
        v_ref[...] = zv
    else:
        k_ref[...] = zk.T
        zvt = zv.T
        v_ref[...] = zvt
        kb_ref[...] = zk.astype(BF16)
        for c in range(rows // MOBA_BLOCK):
            blk = slice(c * MOBA_BLOCK, (c + 1) * MOBA_BLOCK)
            kmean_ref[c] = jnp.sum(zk[blk], axis=0, keepdims=True) * np.float32(1.0 / MOBA_BLOCK)
            vt_ref[c] = zvt[:, blk].astype(BF16)

    sga_ref[...] = _silu(proj(o_ga, ATT_WIDTH)).astype(BF16)

    u = _gelu_tanh(proj(o_u, GM_WIDTH))
    gv = _gelu_tanh(proj(o_vs, GM_WIDTH))
    def group_mean(t):
        t16 = t.astype(BF16)
        return jnp.concatenate([_dot(t16[:, c:c + MXU_COLS], gavg_ref[...])
                                for c in range(0, GM_WIDTH, MXU_COLS)], axis=1)

    mu = group_mean(gv)
    d = gv - mu
    var = group_mean(d * d)
    vn = d * lax.rsqrt(var + LN_EPS) * gsgu_ref[...]
    if sample:
        vn_ref[...] = vn

    r_i = lax.broadcasted_iota(jnp.int32, (GM_CHUNK, 2 * GM_CHUNK), 0)
    c_i = lax.broadcasted_iota(jnp.int32, (GM_CHUNK, 2 * GM_CHUNK), 1)
    tril = r_i >= (c_i & (GM_CHUNK - 1))
    lane_lo = lax.broadcasted_iota(jnp.int32, (GM_CHUNK, LANES), 1) < GM_GROUP_DIM
    sgg = _silu(proj(o_gg, GM_WIDTH))
    n_chunks = rows // GM_CHUNK
    chunks_per_dot = 2 if n_chunks % 2 == 0 else 1
    for p in range(GM_GROUPS // 2):
        w_pair = jnp.where(tril, wcat_ref[p], 0.0).astype(BF16)
        cols = slice(p * LANES, (p + 1) * LANES)

        def split_groups(c):
            vp = vn[c * GM_CHUNK:(c + 1) * GM_CHUNK, cols]
            return jnp.concatenate([jnp.where(lane_lo, vp, 0.0).astype(BF16),
                                    jnp.where(lane_lo, 0.0, vp).astype(BF16)], axis=0)

        for c0 in range(0, n_chunks, chunks_per_dot):
            cs = range(c0, c0 + chunks_per_dot)
            mixed_all = _dot(w_pair, jnp.concatenate([split_groups(c) for c in cs], axis=1))
            for k, c in enumerate(cs):
                rws = slice(c * GM_CHUNK, (c + 1) * GM_CHUNK)
                mixed = mixed_all[:, k * LANES:(k + 1) * LANES] + bias_ref[:, cols]
                gmg_ref[rws, cols] = (u[rws, cols] * mixed * sgg[rws, cols]).astype(BF16)


def _in_proj(x, g_pre, w_in16, g_sgu, gavg, wcat, bias, *, rows, sample, seq=None):
    n, d_model = x.shape
    assert n % rows == 0 and rows % GM_CHUNK == 0
    steps = n // rows
    row_spec = lambda width: pl.BlockSpec((rows, width), lambda i: (i, 0))
    full = lambda a: pl.BlockSpec(a.shape, lambda i: (0,) * a.ndim)
    if sample:
        out_shape = [jax.ShapeDtypeStruct((n, ATT_WIDTH), F32),
                     jax.ShapeDtypeStruct((n, KV_WIDTH), F32),
                     jax.ShapeDtypeStruct((n, KV_WIDTH), F32),
                     jax.ShapeDtypeStruct((n, ATT_WIDTH), BF16),
                     jax.ShapeDtypeStruct((n, GM_WIDTH), BF16),
                     jax.ShapeDtypeStruct((n, GM_WIDTH), F32)]
        out_specs = [row_spec(ATT_WIDTH), row_spec(KV_WIDTH), row_spec(KV_WIDTH),
                     row_spec(ATT_WIDTH), row_spec(GM_WIDTH), row_spec(GM_WIDTH)]
    else:
        assert rows % MOBA_BLOCK == 0 and seq % rows == 0
        bps = rows // MOBA_BLOCK
        nblk = n // MOBA_BLOCK
        spb = seq // rows
        kv_t_spec = pl.BlockSpec((None, KV_WIDTH, rows), lambda i: (i // spb, 0, i % spb))
        out_shape = [jax.ShapeDtypeStruct((n, N_HEADS * LANES), BF16),
                     jax.ShapeDtypeStruct((n // seq, KV_WIDTH, seq), F32),
                     jax.ShapeDtypeStruct((n // seq, KV_WIDTH, seq), F32),
                     jax.ShapeDtypeStruct((n, KV_WIDTH), BF16),
                     jax.ShapeDtypeStruct((nblk, KV_WIDTH, MOBA_BLOCK), BF16),
                     jax.ShapeDtypeStruct((nblk, 1, KV_WIDTH), F32),
                     jax.ShapeDtypeStruct((n, ATT_WIDTH), BF16),
                     jax.ShapeDtypeStruct((n, GM_WIDTH), BF16)]
        out_specs = [row_spec(N_HEADS * LANES), kv_t_spec, kv_t_spec, row_spec(KV_WIDTH),
                     pl.BlockSpec((bps, KV_WIDTH, MOBA_BLOCK), lambda i: (i, 0, 0)),
                     pl.BlockSpec((bps, 1, KV_WIDTH), lambda i: (i, 0, 0)),
                     row_spec(ATT_WIDTH), row_spec(GM_WIDTH)]
    return pl.pallas_call(
        functools.partial(_in_proj_kernel, sample=sample),
        out_shape=out_shape,
        grid=(steps,),
        in_specs=[row_spec(d_model), full(g_pre), full(w_in16), full(g_sgu), full(gavg), full(wcat), full(bias)],
        out_specs=out_specs,
        compiler_params=pltpu.CompilerParams(dimension_semantics=("arbitrary",),
                                             vmem_limit_bytes=VMEM_LIMIT_BYTES),
        name="in_proj_sample" if sample else "in_proj_prompt",
    )(x, g_pre, w_in16, g_sgu, gavg, wcat, bias)


def _tail(att, sga_ref, gmg_ref, x_ref, wout_ref, gpost_ref):
    mix = jnp.concatenate([(att * sga_ref[...].astype(F32)).astype(BF16), gmg_ref[...]], axis=-1)
    out = _dot(mix, wout_ref[...])
    return x_ref[...] + _rms_norm(out, gpost_ref[...])


def _attn_kernel(pt_ref, q_ref, kb_ref, vt_ref, kmean_ref, sga_ref, gmg_ref, x_ref, wout_ref, gpost_ref,
                 qbd_ref, knew_ref, vnew_ref, ck_hbm, cv_hbm,
                 y_ref, sout_ref, sel_ref, g_ref, m_ref, acc_ref, attt_ref, sa_ref, sb_ref, bma_ref, bmb_ref,
                 so_ref, sm_ref, sl_ref, sg_ref, kt16_ref, vt16_ref, kbuf_ref, vbuf_ref, page_sem,
                 *, pages_per_step, steps_per_seq, n_past_blocks):
    i = pl.program_id(1)
    page_size = kbuf_ref.shape[-1]

    step = pl.program_id(0) * pl.num_programs(1) + i
    n_steps = pl.num_programs(0) * pl.num_programs(1)
    slot = step & 1

    def page_copies(for_step, into_slot):
        seq = for_step // steps_per_seq
        first = lax.rem(for_step, jnp.int32(steps_per_seq)) * pages_per_step
        copies = []
        for r in range(pages_per_step):
            page = pt_ref[seq, first + r]
            copies.append(pltpu.make_async_copy(ck_hbm.at[page], kbuf_ref.at[into_slot, r], page_sem.at[into_slot, 0]))
            copies.append(pltpu.make_async_copy(cv_hbm.at[page], vbuf_ref.at[into_slot, r], page_sem.at[into_slot, 1]))
        return copies

    @pl.when(step == 0)
    def _():
        for c in page_copies(step, slot):
            c.start()

    for c in page_copies(step, slot):
        c.wait()
    next_step = jnp.where(step + 1 < n_steps, step + 1, 0)
    _sample_step(qbd_ref, knew_ref, vnew_ref,
                 [kbuf_ref.at[slot, r] for r in range(pages_per_step)],
                 [vbuf_ref.at[slot, r] for r in range(pages_per_step)],
                 sout_ref, so_ref, sm_ref, sl_ref, sg_ref, kt16_ref, vt16_ref,
                 s_idx=lax.rem(step, jnp.int32(steps_per_seq)), n_steps=steps_per_seq,
                 n_past_blocks=n_past_blocks, page_size=page_size)

    nblk = kb_ref.shape[0]
    tq = q_ref.shape[0]
    ones_rows = jnp.ones((L_ROWS, MOBA_BLOCK), BF16)

    def q_head(h):
        return q_ref[:, h * LANES:(h + 1) * LANES]

    def k_tile(j, h):
        a = (h // KV_GROUP) // 2
        return kb_ref[j, :, a * LANES:(a + 1) * LANES]

    def vt_aug(j, h):
        kh = h // KV_GROUP
        return jnp.concatenate([vt_ref[j, kh * HEAD_DIM:(kh + 1) * HEAD_DIM, :], ones_rows], axis=0)

    blk_i = lax.broadcasted_iota(jnp.int32, (nblk, tq), 0)
    km = kmean_ref[...]

    causal = (lax.broadcasted_iota(jnp.int32, (MOBA_BLOCK, tq), 0)
              <= lax.broadcasted_iota(jnp.int32, (MOBA_BLOCK, tq), 1))

    head_pairs = [(KV_GROUP * kh, KV_GROUP * kh + 1) for kh in range(N_KV_HEADS)]
    bufs = ((sa_ref, bma_ref), (sb_ref, bmb_ref))

    def scores(j, mask, buf, heads):
        s_buf, bm_buf = buf
        for h in heads:
            s = _dot_nt(k_tile(j, h), q_head(h))
            if mask is not None:
                s = jnp.where(mask, s, NEG_INF)
            s16 = s.astype(BF16)
            s_buf[h] = s16
            bm_buf[h] = jnp.max(s16, axis=0, keepdims=True).astype(F32)

    def attend_own(buf, heads):
        s_buf, bm_buf = buf
        for h in heads:
            m = bm_buf[h]
            m_ref[h] = m
            acc_ref[h] = _dot(vt_aug(i, h), jnp.exp2(s_buf[h] - m.astype(BF16)))

    def attend(j, buf, heads):
        s_buf, bm_buf = buf
        for h in heads:
            chosen = sel_ref[h, pl.ds(j, 1), :] > 0.5
            m_old = m_ref[h]
            m_new = jnp.where(chosen, jnp.maximum(m_old, bm_buf[h]), m_old)
            p = jnp.exp2(s_buf[h] - jnp.where(chosen, m_new, jnp.inf).astype(BF16))
            acc_ref[h] = acc_ref[h] * jnp.exp2(m_old - m_new) + _dot(vt_aug(j, h), p)
            m_ref[h] = m_new

    def overlapped(attend_pair, j_next, buf_next):
        ahead = 2
        for pair in head_pairs[:ahead]:
            scores(j_next, None, buf_next, pair)
        for k, pair in enumerate(head_pairs):
            attend_pair(pair)
            if k + ahead < len(head_pairs):
                scores(j_next, None, buf_next, head_pairs[k + ahead])

    for c in page_copies(next_step, 1 - slot):
        c.start()

    for h in range(N_HEADS):
        a = (h // KV_GROUP) // 2
        km_a = km[:, a * LANES:(a + 1) * LANES]
        km_hi = km_a.astype(BF16)
        km_lo = (km_a - km_hi.astype(F32)).astype(BF16)
        s_aug = _dot_nt(jnp.concatenate([k_tile(i, h), km_hi, km_lo], axis=0), q_head(h))
        g_ref[h] = s_aug[MOBA_BLOCK:MOBA_BLOCK + nblk] + s_aug[MOBA_BLOCK + nblk:]
        s16 = jnp.where(causal, s_aug[:MOBA_BLOCK], NEG_INF).astype(BF16)
        sa_ref[h] = s16
        bma_ref[h] = jnp.max(s16, axis=0, keepdims=True).astype(F32)

    overlapped(lambda pair: attend_own(bufs[0], pair), 0, bufs[1])

    for h in range(N_HEADS):
        g = g_ref[h]
        rank = jnp.zeros((nblk, tq), F32)
        for n2 in range(nblk):
            gb = g[n2:n2 + 1, :]
            beats = jnp.where(blk_i > n2, jnp.where(g > gb, 0.0, 1.0), jnp.where(gb > g, 1.0, 0.0))
            rank = rank + jnp.where(n2 < i, beats, 0.0)
        sel_ref[h] = jnp.where((blk_i < i) & (rank < MOBA_TOP_K), 1.0, 0.0)

    def past_pair(t, carry):
        j0 = 2 * t
        overlapped(lambda pair: attend(j0, bufs[1], pair), j0 + 1, bufs[0])
        overlapped(lambda pair: attend(j0 + 1, bufs[0], pair), j0 + 2, bufs[1])
        return carry

    lax.fori_loop(0, lax.shift_right_logical(i, jnp.int32(1)), past_pair, 0)

    @pl.when((i & 1) == 1)
    def _():
        attend(i - 1, bufs[1], range(N_HEADS))

    for h in range(N_HEADS):
        acc = acc_ref[h]
        attt_ref[h * HEAD_DIM:(h + 1) * HEAD_DIM, :] = acc[:HEAD_DIM] / acc[HEAD_DIM:HEAD_DIM + 1]
    y_ref[...] = _tail(attt_ref[...].T, sga_ref, gmg_ref, x_ref, wout_ref, gpost_ref)

    @pl.when(step == n_steps - 1)
    def _():
        for c in page_copies(next_step, 1 - slot):
            c.wait()


def _attn(page_table, qpad, kb, vt, kmean, sga, gmg, x, w_out16, g_post, qbd, k_new, v_new, cache_k, cache_v,
          *, batch, seq):
    nblk = seq // MOBA_BLOCK
    d_model = x.shape[-1]
    steps = batch * nblk
    db, n_pages = page_table.shape
    page_size = cache_k.shape[2]
    nq = qbd.shape[1]
    t_new = k_new.shape[1]
    pages_per_block = MOBA_BLOCK // page_size
    n_past_blocks = n_pages // pages_per_block
    assert (db * n_pages) % steps == 0
    pps = db * n_pages // steps
    assert n_pages % pps == 0 and pps % pages_per_block == 0
    spq = n_pages // pps
    assert n_past_blocks <= LANES and t_new <= LANES

    tile = lambda width: pl.BlockSpec((MOBA_BLOCK, width), lambda b, i, pt: (b * nblk + i, 0))
    per_batch = lambda r, c: pl.BlockSpec((None, nblk, r, c), lambda b, i, pt: (b, 0, 0, 0))
    full = lambda a: pl.BlockSpec(a.shape, lambda b, i, pt: (0,) * a.ndim)
    per_seq = lambda rows: pl.BlockSpec((None, rows, KV_WIDTH), lambda b, i, pt: ((b * nblk + i) // spq, 0, 0))

    kernel = functools.partial(_attn_kernel, pages_per_step=pps, steps_per_seq=spq, n_past_blocks=n_past_blocks)
    blocks_per_step = pps // pages_per_block
    return pl.pallas_call(
        kernel,
        out_shape=[jax.ShapeDtypeStruct(x.shape, F32), jax.ShapeDtypeStruct((db, nq, KV_WIDTH), F32)],
        grid_spec=pltpu.PrefetchScalarGridSpec(
            num_scalar_prefetch=1,
            grid=(batch, nblk),
            in_specs=([tile(N_HEADS * LANES),
                       per_batch(MOBA_BLOCK, KV_WIDTH),
                       per_batch(KV_WIDTH, MOBA_BLOCK),
                       pl.BlockSpec((None, nblk, KV_WIDTH), lambda b, i, pt: (b, 0, 0)),
                       tile(ATT_WIDTH), tile(GM_WIDTH), tile(d_model), full(w_out16), full(g_post),
                       per_seq(nq), per_seq(t_new), per_seq(t_new),
                       pl.BlockSpec(memory_space=pl.ANY), pl.BlockSpec(memory_space=pl.ANY)]),
            out_specs=[tile(d_model), per_seq(nq)],
            scratch_shapes=[pltpu.VMEM((N_HEADS, nblk, MOBA_BLOCK), F32),
                            pltpu.VMEM((N_HEADS, nblk, MOBA_BLOCK), F32),
                            pltpu.VMEM((N_HEADS, 1, MOBA_BLOCK), F32),
                            pltpu.VMEM((N_HEADS, HEAD_DIM + L_ROWS, MOBA_BLOCK), F32),
                            pltpu.VMEM((ATT_WIDTH, MOBA_BLOCK), F32),
                            pltpu.VMEM((N_HEADS, MOBA_BLOCK, MOBA_BLOCK), BF16),
                            pltpu.VMEM((N_HEADS, MOBA_BLOCK, MOBA_BLOCK), BF16),
                            pltpu.VMEM((N_HEADS, 1, MOBA_BLOCK), F32),
                            pltpu.VMEM((N_HEADS, 1, MOBA_BLOCK), F32),
                            pltpu.VMEM((n_past_blocks, nq, KV_WIDTH), F32),
                            pltpu.VMEM((nq, LANES), F32),
                            pltpu.VMEM((nq, LANES), F32),
                            pltpu.VMEM((nq, LANES), F32),
                            pltpu.VMEM((blocks_per_step, KV_WIDTH, MOBA_BLOCK), BF16),
                            pltpu.VMEM((blocks_per_step, KV_WIDTH, MOBA_BLOCK), BF16),
                            pltpu.VMEM((2, pps, KV_WIDTH, page_size), F32),
                            pltpu.VMEM((2, pps, KV_WIDTH, page_size), F32),
                            pltpu.SemaphoreType.DMA((2, 2))]),
        compiler_params=pltpu.CompilerParams(dimension_semantics=("arbitrary", "arbitrary"),
                                             vmem_limit_bytes=VMEM_LIMIT_BYTES),
        name="attn",
    )(page_table, qpad, kb.reshape(batch, nblk, MOBA_BLOCK, KV_WIDTH), vt.reshape(batch, nblk, KV_WIDTH, MOBA_BLOCK),
      kmean.reshape(batch, nblk, KV_WIDTH), sga, gmg, x, w_out16, g_post, qbd, k_new, v_new, cache_k, cache_v)


def _sample_step(qbd_ref, knew_ref, vnew_ref, k_pages, v_pages, out_ref, o_ref, m_ref, l_ref, g_ref,
                 kt16_ref, vt16_ref, *, s_idx, n_steps, n_past_blocks, page_size):
    pages_per_block = MOBA_BLOCK // page_size
    blocks_per_step = len(k_pages) // pages_per_block
    nq = qbd_ref.shape[0]
    t_new = knew_ref.shape[0]
    lane = lax.broadcasted_iota(jnp.int32, (nq, LANES), 1)

    qbd = qbd_ref[...]
    qbd16 = qbd.astype(BF16)
    for jj in range(blocks_per_step):
        for r in range(pages_per_block):
            cols = slice(r * page_size, (r + 1) * page_size)
            kt16_ref[jj, :, cols] = k_pages[jj * pages_per_block + r][...].astype(BF16)
            vt16_ref[jj, :, cols] = v_pages[jj * pages_per_block + r][...].astype(BF16)
    s_blk = [_dot(qbd16, kt16_ref[jj]) for jj in range(blocks_per_step)]
    m_blk = [jnp.max(s, axis=-1, keepdims=True) for s in s_blk]
    g_blk = [jnp.sum(s, axis=-1, keepdims=True) * np.float32(1.0 / MOBA_BLOCK) for s in s_blk]
    p_blk = [jnp.exp2(s - m) for s, m in zip(s_blk, m_blk)]
    l_blk = [jnp.sum(p, axis=-1, keepdims=True) for p in p_blk]
    fresh = s_idx == 0
    m_all = jnp.where(fresh, NEG_INF, m_ref[...])
    l_all = jnp.where(fresh, 0.0, l_ref[...])
    g_all = jnp.where(fresh, NEG_INF, g_ref[...])
    for jj in range(blocks_per_step):
        blk = s_idx * blocks_per_step + jj
        o_ref[blk] = _dot_nt(p_blk[jj].astype(BF16), vt16_ref[jj])
        hit = lane == blk
        m_all = jnp.where(hit, m_blk[jj], m_all)
        l_all = jnp.where(hit, l_blk[jj], l_all)
        g_all = jnp.where(hit, g_blk[jj], g_all)
    m_ref[...], l_ref[...], g_ref[...] = m_all, l_all, g_all

    @pl.when(s_idx == n_steps - 1)
    def _():
        g = g_ref[...]
        lane_f = lane.astype(F32)
        sel = jnp.zeros((nq, LANES), jnp.bool_)
        for _ in range(min(MOBA_TOP_K, n_past_blocks)):
            best = jnp.max(g, axis=-1, keepdims=True)
            idx = jnp.min(jnp.where(g == best, lane_f, np.float32(LANES)), axis=-1, keepdims=True)
            pick = lane_f == idx
            sel = sel | pick
            g = jnp.where(pick, NEG_INF, g)
        row = lax.broadcasted_iota(jnp.int32, (nq, LANES), 0)
        rows_per_token = nq // t_new
        s_own = jnp.full((nq, LANES), NEG_INF, F32)
        for t in range(t_new):
            s_t = jnp.sum(qbd * knew_ref[t:t + 1, :], axis=-1, keepdims=True)
            s_own = jnp.where((lane == t) & (row >= t * rows_per_token), s_t, s_own)
        m_blocks = m_ref[...]
        m_tot = jnp.maximum(jnp.max(jnp.where(sel, m_blocks, NEG_INF), axis=-1, keepdims=True),
                            jnp.max(s_own, axis=-1, keepdims=True))
        w = jnp.where(sel, jnp.exp2(m_blocks - m_tot), 0.0)
        p_own = jnp.exp2(s_own - m_tot)
        l_tot = (jnp.sum(w * l_ref[...], axis=-1, keepdims=True) + jnp.sum(p_own, axis=-1, keepdims=True))
        acc = jnp.zeros((nq, KV_WIDTH), F32)
        for n in range(n_past_blocks):
            acc = acc + jnp.sum(jnp.where(lane == n, w, 0.0), axis=-1, keepdims=True) * o_ref[n]
        for t in range(t_new):
            acc = acc + jnp.sum(jnp.where(lane == t, p_own, 0.0), axis=-1, keepdims=True) * vnew_ref[t:t + 1, :]
        out_ref[...] = acc / l_tot


def _sample_tail_kernel(att_ref, sga_ref, gmg_ref, x_ref, wout_ref, gpost_ref, y_ref):
    y_ref[...] = _tail(att_ref[...], sga_ref, gmg_ref, x_ref, wout_ref, gpost_ref)


def _sample_tail(att, sga, gmg, x, w_out16, g_post):
    return pl.pallas_call(
        _sample_tail_kernel,
        out_shape=jax.ShapeDtypeStruct(x.shape, F32),
        compiler_params=pltpu.CompilerParams(vmem_limit_bytes=VMEM_LIMIT_BYTES),
        name="sample_tail",
    )(att, sga, gmg, x, w_out16, g_post)


def _pair_cat(w):
    g, c, _ = w.shape
    return w.reshape(g // 2, 2, c, c).transpose(0, 2, 1, 3).reshape(g // 2, c, 2 * c)


def _layer(x_p, x_s, cache_k, cache_v, page_table, g_pre, w_in, g_sgu, w_sp, b_sp, w_out, g_post):
    batch, seq, d_model = x_p.shape
    db, t_new, _ = x_s.shape
    assert seq % MOBA_BLOCK == 0 and GM_CHUNK % t_new == 0
    w_in16 = w_in.astype(BF16)
    w_out16 = w_out.astype(BF16)
    g_pre2, g_sgu2, g_post2 = g_pre[None, :], g_sgu[None, :], g_post[None, :]
    grp = jnp.arange(MXU_COLS) // GM_GROUP_DIM
    gavg = jnp.where(grp[:, None] == grp[None, :], 1.0 / GM_GROUP_DIM, 0.0).astype(BF16)
    bias_p = jnp.repeat(b_sp.T, GM_GROUP_DIM, axis=1)
    wcat_p = _pair_cat(w_sp)
    seqs = GM_CHUNK // t_new
    w_s = jnp.einsum('ab,gts->gatbs', jnp.eye(seqs, dtype=F32), w_sp[:, :t_new, :t_new])
    wcat_s = _pair_cat(w_s.reshape(GM_GROUPS, GM_CHUNK, GM_CHUNK))
    bias_s = jnp.tile(bias_p[:t_new], (seqs, 1))

    xp2 = x_p.reshape(batch * seq, d_model)
    qpad, kt_p, vt_p, kb, vt, kmean, sga_p, gmg_p = _in_proj(
        xp2, g_pre2, w_in16, g_sgu2, gavg, wcat_p, bias_p, rows=PROMPT_ROWS, sample=False, seq=seq)

    xs2 = x_s.reshape(db * t_new, d_model)
    q_s, k_s, v_s, sga_s, gmg_s, vn_s = _in_proj(
        xs2, g_pre2, w_in16, g_sgu2, gavg, wcat_s, bias_s, rows=db * t_new, sample=True)
    q5 = q_s.reshape(db, t_new, N_KV_HEADS, KV_GROUP, HEAD_DIM)
    eye = jnp.eye(N_KV_HEADS, dtype=F32)[None, None, :, None, :, None]
    qbd = (q5[:, :, :, :, None, :] * eye).reshape(db, t_new * N_HEADS, KV_WIDTH)
    n_phys, page_size = cache_k.shape[:2]
    ck = cache_k.transpose(0, 2, 3, 1).reshape(n_phys, KV_WIDTH, page_size)
    cv = cache_v.transpose(0, 2, 3, 1).reshape(n_phys, KV_WIDTH, page_size)
    y_p, o_full = _attn(page_table, qpad, kb, vt, kmean, sga_p, gmg_p, xp2, w_out16, g_post2, qbd,
                        k_s.reshape(db, t_new, KV_WIDTH), v_s.reshape(db, t_new, KV_WIDTH), ck, cv,
                        batch=batch, seq=seq)
    o6 = o_full.reshape(db, t_new, N_KV_HEADS, KV_GROUP, N_KV_HEADS, HEAD_DIM)
    att_s = jnp.stack([o6[:, :, kh, :, kh, :] for kh in range(N_KV_HEADS)], axis=2)
    att_s = att_s.reshape(db * t_new, ATT_WIDTH)
    y_s = _sample_tail(att_s, sga_s, gmg_s, xs2, w_out16, g_post2)

    rows_of = lambda t: t.reshape(batch, N_KV_HEADS, HEAD_DIM, seq).transpose(0, 3, 1, 2)
    kv_s = (db, t_new, N_KV_HEADS, HEAD_DIM)
    return (y_p.reshape(batch, seq, d_model), y_s.reshape(db, t_new, d_model),
            rows_of(kt_p), rows_of(vt_p), k_s.reshape(kv_s), v_s.reshape(kv_s),
            vn_s.reshape(db, t_new, GM_WIDTH))


def kernel(x_prompt, x_sample, cache_k, cache_v, page_table, g_pre, w_in, g_sgu, w_spatial, b_spatial, w_out, g_post):
    yp, ys = x_prompt, x_sample
    outs = []
    for l in range(w_in.shape[0]):
        yp, ys, kp, vp, kn, vn, sv = _layer(yp, ys, cache_k[l], cache_v[l], page_table, g_pre[l], w_in[l],
                                            g_sgu[l], w_spatial[l], b_spatial[l], w_out[l], g_post[l])
        outs.append((kp, vp, kn, vn, sv))
    stacked = [jnp.stack(parts) for parts in zip(*outs)]
    return (yp, ys, *stacked)
```
